```python
import jax, jax.numpy as jnp
from jax import lax
import numpy as np

D_MODEL = 4096
BATCH = 2
SEQ = 4096
DEPTH = 2

CHUNK = 64
Q_BLOCK = 128
EPS = 1e-6
MIX_WIDTH = D_MODEL
POOL_WINDOWS = (2, 4, 8, 16)
POOL_GROUPS = len(POOL_WINDOWS)
POOL_WIDTH = MIX_WIDTH // 2
POOL_GROUP_WIDTH = POOL_WIDTH // POOL_GROUPS
SG_WIDTH = MIX_WIDTH - POOL_WIDTH
SG_HEADS = 8
SG_HEAD_DIM = SG_WIDTH // SG_HEADS
SG_LEN = 128
SB_HEAD_DIM = 128
SB_HEADS = D_MODEL // SB_HEAD_DIM
SB_SCALE = SB_HEAD_DIM ** -0.5
D_FF = 256 * ((8 * D_MODEL // 3 + 255) // 256)
N_EXPERTS = 8
TOP_K = 2
EXPERT_FF = D_FF // 2
N_EVEN = (DEPTH + 1) // 2
N_ODD = DEPTH // 2

kernel_name = "hybrid_pool_gmlp_stickbreak_moe_encoder"


def rms_norm(x, gain):
    xf = x.astype(jnp.float32)
    y = xf * lax.rsqrt(jnp.mean(xf * xf, axis=-1, keepdims=True) + EPS)
    return (y * gain.astype(jnp.float32)).astype(x.dtype)


def swiglu(h, w1, w3, w2):
    a = jnp.einsum("bsd,df->bsf", h, w1)
    g = jnp.einsum("bsd,df->bsf", h, w3)
    return jnp.einsum("bsf,fd->bsd", jax.nn.silu(a) * g, w2)


def multiscale_pool(za, pool_w, pool_scale):
    seq = za.shape[1]
    zf = za.astype(jnp.float32)
    csum = jnp.cumsum(zf, axis=1)
    n_avail = jnp.arange(1, seq + 1, dtype=jnp.float32)[None, :, None]
    outs = []
    for g, w in enumerate(POOL_WINDOWS):
        sl = slice(g * POOL_GROUP_WIDTH, (g + 1) * POOL_GROUP_WIDTH)
        cg = csum[..., sl]
        lagged = jnp.pad(cg, ((0, 0), (w, 0), (0, 0)))[:, :seq]
        mean = (cg - lagged) / jnp.minimum(n_avail, float(w))
        pooled = (mean - zf[..., sl]).astype(za.dtype)
        outs.append(jnp.einsum("bsc,cd->bsd", pooled, pool_w[g]))
    return jnp.concatenate(outs, axis=-1) * pool_scale


def spatial_gate(v, sg_w, sg_b):
    b, s, _ = v.shape
    vc = v.reshape(b, s // SG_LEN, SG_LEN, SG_HEADS, SG_HEAD_DIM)
    pos = jnp.arange(SG_LEN)
    allowed = (pos[None, :] // CHUNK) <= (pos[:, None] // CHUNK)
    w = jnp.where(allowed[None], sg_w, 0.0)
    mixed = jnp.einsum("gij,bcjgd->bcigd", w, vc) + sg_b.T[None, None, :, :, None]
    return mixed.reshape(b, s, SG_WIDTH)


def stick_breaking_attention(q, k, v):
    seq = q.shape[2]
    outs = []
    for blk in range(seq // Q_BLOCK):
        q0 = blk * Q_BLOCK
        kend = q0 + Q_BLOCK
        z = jnp.einsum("bhtd,bhsd->bhts", q[:, :, q0:kend], k[:, :, :kend]).astype(jnp.float32) * SB_SCALE
        t_pos = q0 + jnp.arange(Q_BLOCK)[:, None]
        s_pos = jnp.arange(kend)[None, :]
        strict = s_pos < t_pos
        log_keep = jnp.where(strict, -jax.nn.softplus(z), 0.0)
        after = lax.cumsum(log_keep, axis=3, reverse=True) - log_keep
        attn = jnp.where(strict, jnp.exp(jax.nn.log_sigmoid(z) + after), 0.0)
        outs.append(jnp.einsum("bhts,bhsd->bhtd", attn.astype(v.dtype), v[:, :, :kend]))
    return jnp.concatenate(outs, axis=2)


def moe_swiglu(h, router, w1, w3, w2):
    logits = jnp.einsum("bsd,de->bse", h, router).astype(jnp.float32)
    top_val, top_idx = lax.top_k(logits, TOP_K)
    top_w = jax.nn.softmax(top_val, axis=-1)
    gates = jnp.sum(jax.nn.one_hot(top_idx, N_EXPERTS, dtype=jnp.float32) * top_w[..., None], axis=-2)
    out = jnp.zeros_like(h)
    for e in range(N_EXPERTS):
        out = out + gates[..., e:e + 1].astype(h.dtype) * swiglu(h, w1[e], w3[e], w2[e])
    return out


def even_layer(x, norm1, w_in, pool_w, pool_scale, sg_norm, sg_w, sg_b, w_out, norm2, w1, w3, w2):
    h = rms_norm(x, norm1)
    z = jnp.einsum("bsd,dn->bsn", h, w_in)
    za = z[..., :POOL_WIDTH]
    zb = jax.nn.gelu(z[..., POOL_WIDTH:])
    u, v = zb[..., :SG_WIDTH], zb[..., SG_WIDTH:]
    ya = multiscale_pool(za, pool_w, pool_scale)
    yb = u * spatial_gate(rms_norm(v, sg_norm), sg_w, sg_b)
    x = x + jnp.einsum("bsn,nd->bsd", jnp.concatenate([ya, yb], axis=-1), w_out)
    return x + swiglu(rms_norm(x, norm2), w1, w3, w2)


def odd_layer(x, norm1, w_qkv, q_norm, k_norm, w_out, norm2, router, w1, w3, w2):
    b, s, _ = x.shape
    h = rms_norm(x, norm1)
    qkv = jnp.einsum("bsd,dn->bsn", h, w_qkv).reshape(b, s, 3, SB_HEADS, SB_HEAD_DIM)
    q = rms_norm(qkv[:, :, 0], q_norm).transpose(0, 2, 1, 3)
    k = rms_norm(qkv[:, :, 1], k_norm).transpose(0, 2, 1, 3)
    v = qkv[:, :, 2].transpose(0, 2, 1, 3)
    o = stick_breaking_attention(q, k, v).transpose(0, 2, 1, 3).reshape(b, s, D_MODEL)
    x = x + jnp.einsum("bsn,nd->bsd", o, w_out)
    return x + moe_swiglu(rms_norm(x, norm2), router, w1, w3, w2)


def setup_inputs(seed: int = 0) -> dict:
    key = jax.random.key(seed)
    ks = jax.random.split(key, 24)

    def nrm(k, shape, scale):
        return jax.random.normal(k, shape, jnp.float32) * scale

    def gain(k, shape):
        return 1.0 + 0.02 * jax.random.normal(k, shape, jnp.float32)

    in_width = POOL_WIDTH + 2 * SG_WIDTH
    return {
        "x": nrm(ks[0], (BATCH, SEQ, D_MODEL), 1.0),
        "e_norm1": gain(ks[1], (N_EVEN, D_MODEL)),
        "e_w_in": nrm(ks[2], (N_EVEN, D_MODEL, in_width), D_MODEL ** -0.5),
        "e_pool_w": nrm(ks[3], (N_EVEN, POOL_GROUPS, POOL_GROUP_WIDTH, POOL_GROUP_WIDTH), POOL_GROUP_WIDTH ** -0.5),
        "e_pool_scale": gain(ks[4], (N_EVEN, POOL_WIDTH)),
        "e_sg_norm": gain(ks[5], (N_EVEN, SG_WIDTH)),
        "e_sg_w": nrm(ks[6], (N_EVEN, SG_HEADS, SG_LEN, SG_LEN), SG_LEN ** -0.5),
        "e_sg_b": gain(ks[7], (N_EVEN, SG_HEADS, SG_LEN)),
        "e_w_out": nrm(ks[8], (N_EVEN, MIX_WIDTH, D_MODEL), MIX_WIDTH ** -0.5),
        "e_norm2": gain(ks[9], (N_EVEN, D_MODEL)),
        "e_ffn_w1": nrm(ks[10], (N_EVEN, D_MODEL, D_FF), D_MODEL ** -0.5),
        "e_ffn_w3": nrm(ks[11], (N_EVEN, D_MODEL, D_FF), D_MODEL ** -0.5),
        "e_ffn_w2": nrm(ks[12], (N_EVEN, D_FF, D_MODEL), D_FF ** -0.5),
        "o_norm1": gain(ks[13], (N_ODD, D_MODEL)),
        "o_w_qkv": nrm(ks[14], (N_ODD, D_MODEL, 3 * D_MODEL), D_MODEL ** -0.5),
        "o_q_norm": gain(ks[15], (N_ODD, SB_HEAD_DIM)),
        "o_k_norm": gain(ks[16], (N_ODD, SB_HEAD_DIM)),
        "o_w_out": nrm(ks[17], (N_ODD, D_MODEL, D_MODEL), D_MODEL ** -0.5),
        "o_norm2": gain(ks[18], (N_ODD, D_MODEL)),
        "o_router": nrm(ks[19], (N_ODD, D_MODEL, N_EXPERTS), D_MODEL ** -0.5),
        "o_moe_w1": nrm(ks[20], (N_ODD, N_EXPERTS, D_MODEL, EXPERT_FF), D_MODEL ** -0.5),
        "o_moe_w3": nrm(ks[21], (N_ODD, N_EXPERTS, D_MODEL, EXPERT_FF), D_MODEL ** -0.5),
        "o_moe_w2": nrm(ks[22], (N_ODD, N_EXPERTS, EXPERT_FF, D_MODEL), EXPERT_FF ** -0.5),
    }


def reference(x, e_norm1, e_w_in, e_pool_w, e_pool_scale, e_sg_norm, e_sg_w, e_sg_b, e_w_out,
              e_norm2, e_ffn_w1, e_ffn_w3, e_ffn_w2, o_norm1, o_w_qkv, o_q_norm, o_k_norm,
              o_w_out, o_norm2, o_router, o_moe_w1, o_moe_w3, o_moe_w2):
    for layer in range(DEPTH):
        i = layer // 2
        if layer % 2 == 0:
            x = even_layer(x, e_norm1[i], e_w_in[i], e_pool_w[i], e_pool_scale[i], e_sg_norm[i],
                           e_sg_w[i], e_sg_b[i], e_w_out[i], e_norm2[i], e_ffn_w1[i], e_ffn_w3[i], e_ffn_w2[i])
        else:
            x = odd_layer(x, o_norm1[i], o_w_qkv[i], o_q_norm[i], o_k_norm[i], o_w_out[i], o_norm2[i],
                          o_router[i], o_moe_w1[i], o_moe_w3[i], o_moe_w2[i])
    return x
```

```python
import functools

import jax
import jax.numpy as jnp
from jax import lax
from jax.experimental import pallas as pl
from jax.experimental.pallas import tpu as pltpu

EPS = 1e-6
CHUNK = 64
SG_LEN = 128
SG_HEADS = 8
POOL_WINDOWS = (2, 4, 8, 16)
POOL_HALO = 16
SB_HEAD_DIM = 128
TOP_K = 2

V7X_VMEM_BYTES = 64 * 1024 * 1024
VMEM_LIMIT_BYTES = V7X_VMEM_BYTES - 8 * 1024 * 1024
LANES = 128

F32 = jnp.float32
BF16 = jnp.bfloat16


def _params(*semantics):
    return pltpu.CompilerParams(dimension_semantics=semantics, vmem_limit_bytes=VMEM_LIMIT_BYTES)


def _rmsnorm_kernel(x_ref, g_ref, o_ref):
    x = x_ref[...]
    ms = jnp.mean(x * x, axis=-1, keepdims=True)
    o_ref[...] = (x * lax.rsqrt(ms + EPS) * g_ref[...]).astype(o_ref.dtype)


def rmsnorm(x, gain, tm=256):
    m, d = x.shape
    tm = min(tm, m)
    return pl.pallas_call(
        _rmsnorm_kernel,
        grid=(m // tm,),
        in_specs=[pl.BlockSpec((tm, d), lambda i: (i, 0)),
                  pl.BlockSpec((1, d), lambda i: (0, 0))],
        out_specs=pl.BlockSpec((tm, d), lambda i: (i, 0)),
        out_shape=jax.ShapeDtypeStruct((m, d), BF16),
        compiler_params=_params("arbitrary"),
        name="rmsnorm",
    )(x, gain.reshape(1, d))


def _gmm_kernel(te_ref, tv_ref, x_ref, *refs, n_w, n_extra, epilogue):
    del te_ref
    w_refs = refs[:n_w]
    extra_refs = refs[n_w:n_w + n_extra]
    o_ref = refs[n_w + n_extra]
    j = pl.program_id(0)
    i = pl.program_id(1)

    @pl.when(tv_ref[i] != 0)
    def _():
        x = x_ref[...]
        accs = [jnp.dot(x, w[0].astype(BF16), preferred_element_type=F32) for w in w_refs]
        epilogue(accs, extra_refs, o_ref, j)

    @pl.when(tv_ref[i] == 0)
    def _():
        o_ref[...] = jnp.zeros(o_ref.shape, o_ref.dtype)


def gmm(x, weights, tile_expert, tile_valid, *, tm, tn, epilogue, out_dtype,
        extras=(), extra_specs=(), x_col_block=0, name="gmm"):
    p = x.shape[0]
    _, k, n = weights[0].shape
    grid = (pl.cdiv(n, tn), p // tm)
    x_spec = pl.BlockSpec((tm, k), lambda j, i, te, tv: (i, x_col_block))
    w_spec = pl.BlockSpec((1, k, tn), lambda j, i, te, tv: (te[i], 0, j))
    kern = functools.partial(_gmm_kernel, n_w=len(weights), n_extra=len(extras), epilogue=epilogue)
    return pl.pallas_call(
        kern,
        grid_spec=pltpu.PrefetchScalarGridSpec(
            num_scalar_prefetch=2,
            grid=grid,
            in_specs=[x_spec] + [w_spec] * len(weights) + list(extra_specs),
            out_specs=pl.BlockSpec((tm, tn), lambda j, i, te, tv: (i, j)),
        ),
        out_shape=jax.ShapeDtypeStruct((p, n), out_dtype),
        compiler_params=_params("arbitrary", "arbitrary"),
        name=name,
    )(tile_expert, tile_valid, x, *weights, *extras)


def _dense_tiles(m, tm):
    n = m // tm
    return jnp.zeros((n,), jnp.int32), jnp.ones((n,), jnp.int32)


def _tile_spec(tm, tn):
    return pl.BlockSpec((tm, tn), lambda j, i, te, tv: (i, j))


def _ep_gelu_cols(first_gelu_block, accs, extra_refs, o_ref, j):
    del extra_refs
    acc = accs[0]

    @pl.when(j < first_gelu_block)
    def _():
        o_ref[...] = acc.astype(o_ref.dtype)

    @pl.when(j >= first_gelu_block)
    def _():
        o_ref[...] = jax.nn.gelu(acc).astype(o_ref.dtype)


def _ep_residual(accs, extra_refs, o_ref, j):
    del j
    o_ref[...] = (extra_refs[0][...] + accs[0]).astype(o_ref.dtype)


def _ep_swiglu(accs, extra_refs, o_ref, j):
    del extra_refs, j
    a, g = accs
    o_ref[...] = (jax.nn.silu(a) * g).astype(o_ref.dtype)


def _ep_rowscale(accs, extra_refs, o_ref, j):
    del j
    o_ref[...] = (accs[0] * extra_refs[0][...]).astype(o_ref.dtype)


def _ep_qkv(q_blocks, k_blocks, accs, extra_refs, o_ref, j):
    acc = accs[0]
    qg_ref, kg_ref = extra_refs
    tn = acc.shape[1]

    @pl.when(j < q_blocks + k_blocks)
    def _():
        gain = jnp.where(j < q_blocks, qg_ref[...], kg_ref[...])
        for c in range(tn // SB_HEAD_DIM):
            sl = slice(c * SB_HEAD_DIM, (c + 1) * SB_HEAD_DIM)
            a = acc[:, sl]
            ms = jnp.mean(a * a, axis=-1, keepdims=True)
            o_ref[:, sl] = (a * lax.rsqrt(ms + EPS) * gain).astype(o_ref.dtype)

    @pl.when(j >= q_blocks + k_blocks)
    def _():
        o_ref[...] = acc.astype(o_ref.dtype)


def _mixer_kernel(za_ref, halo_ref, u_ref, v_ref, pw_ref, ps_ref, sgn_ref, sgw_ref, sgb_ref,
                  o_ref, zbuf, *, tiles_per_seq):
    i = pl.program_id(0)
    ts, pool_width = za_ref.shape
    sg_width = v_ref.shape[1]
    group_width = pool_width // len(POOL_WINDOWS)
    head_dim = sg_width // SG_HEADS
    tile_in_seq = i % tiles_per_seq

    za = za_ref[...]
    zbuf[POOL_HALO:POOL_HALO + ts, :] = za
    zbuf[0:POOL_HALO, :] = jnp.where(tile_in_seq == 0, 0.0, halo_ref[...])
    n_avail = (lax.broadcasted_iota(jnp.int32, (ts, 1), 0) + tile_in_seq * ts + 1).astype(F32)
    for g, w in enumerate(POOL_WINDOWS):
        cols = slice(g * group_width, (g + 1) * group_width)
        win = za[:, cols]
        for back in range(1, w):
            win = win + zbuf[POOL_HALO - back:POOL_HALO - back + ts, cols]
        mean = win / jnp.minimum(n_avail, float(w))
        pooled = (mean - za[:, cols]).astype(BF16)
        ya = jnp.dot(pooled, pw_ref[g], preferred_element_type=F32) * ps_ref[:, cols]
        o_ref[:, cols] = ya.astype(o_ref.dtype)

    v = v_ref[...]
    ms = jnp.mean(v * v, axis=-1, keepdims=True)
    vn = (v * lax.rsqrt(ms + EPS) * sgn_ref[...]).astype(BF16)
    pos_i = lax.broadcasted_iota(jnp.int32, (SG_LEN, SG_LEN), 0)
    pos_j = lax.broadcasted_iota(jnp.int32, (SG_LEN, SG_LEN), 1)
    allowed = (pos_j // CHUNK) <= (pos_i // CHUNK)
    for h in range(SG_HEADS):
        w_h = jnp.where(allowed, sgw_ref[h], 0.0).astype(BF16)
        hc = slice(h * head_dim, (h + 1) * head_dim)
        for c in range(ts // SG_LEN):
            rows = slice(c * SG_LEN, (c + 1) * SG_LEN)
            mixed = jnp.dot(w_h, vn[rows, hc], preferred_element_type=F32) + sgb_ref[:, hc]
            yb = u_ref[rows, hc] * mixed
            o_ref[rows, pool_width + h * head_dim:pool_width + (h + 1) * head_dim] = yb.astype(o_ref.dtype)


def mixer(z, pool_w, pool_scale, sg_norm, sg_w, sg_b, seq, ts=256):
    m = z.shape[0]
    pool_width = pool_scale.shape[0]
    sg_width = sg_norm.shape[0]
    assert pool_width == sg_width
    ts = min(ts, seq)
    tiles_per_seq = seq // ts
    halo_blocks_per_tile = ts // POOL_HALO
    head_dim = sg_width // SG_HEADS
    bias = jnp.repeat(sg_b.T, head_dim, axis=1)
    kern = functools.partial(_mixer_kernel, tiles_per_seq=tiles_per_seq)
    const2 = lambda i: (0, 0)
    const3 = lambda i: (0, 0, 0)
    return pl.pallas_call(
        kern,
        grid=(m // ts,),
        in_specs=[
            pl.BlockSpec((ts, pool_width), lambda i: (i, 0)),
            pl.BlockSpec((POOL_HALO, pool_width), lambda i: (jnp.maximum(i * halo_blocks_per_tile - 1, 0), 0)),
            pl.BlockSpec((ts, sg_width), lambda i: (i, 1)),
            pl.BlockSpec((ts, sg_width), lambda i: (i, 2)),
            pl.BlockSpec(pool_w.shape, const3),
            pl.BlockSpec((1, pool_width), const2),
            pl.BlockSpec((1, sg_width), const2),
            pl.BlockSpec(sg_w.shape, const3),
            pl.BlockSpec(bias.shape, const2),
        ],
        out_specs=pl.BlockSpec((ts, pool_width + sg_width), lambda i: (i, 0)),
        out_shape=jax.ShapeDtypeStruct((m, pool_width + sg_width), BF16),
        scratch_shapes=[pltpu.VMEM((ts + POOL_HALO, pool_width), F32)],
        compiler_params=_params("arbitrary"),
        name="mixer",
    )(z, z, z, z, pool_w.astype(BF16), pool_scale.reshape(1, -1), sg_norm.reshape(1, -1), sg_w, bias)


def _sb_block(q, k_blk, v_blk, carry, tri, scale, diagonal):
    z = lax.dot_general(q, k_blk, (((1,), (1,)), ((), ())), preferred_element_type=F32) * scale
    softplus = jnp.maximum(z, 0.0) + jnp.log(1.0 + jnp.exp(-jnp.abs(z)))
    log_keep = -softplus
    if diagonal:
        t_pos = lax.broadcasted_iota(jnp.int32, z.shape, 0)
        s_pos = lax.broadcasted_iota(jnp.int32, z.shape, 1)
        strict = s_pos < t_pos
        log_keep = jnp.where(strict, log_keep, 0.0)
    after = jnp.dot(log_keep.astype(BF16), tri, preferred_element_type=F32) + carry
    attn = jnp.exp(z - softplus + after)
    if diagonal:
        attn = jnp.where(strict, attn, 0.0)
    out = jnp.dot(attn.astype(BF16), v_blk, preferred_element_type=F32)
    return out, carry + jnp.sum(log_keep, axis=1, keepdims=True)


def _attn_kernel(q_ref, k_ref, v_ref, o_ref, *, scale):
    qi = pl.program_id(2)
    tq = q_ref.shape[0]
    q = q_ref[...]
    j_pos = lax.broadcasted_iota(jnp.int32, (tq, tq), 0)
    s_pos = lax.broadcasted_iota(jnp.int32, (tq, tq), 1)
    tri = (j_pos > s_pos).astype(BF16)

    d0 = pl.multiple_of(qi * tq, tq)
    acc, carry = _sb_block(q, k_ref[pl.ds(d0, tq), :], v_ref[pl.ds(d0, tq), :],
                           jnp.zeros((tq, 1), F32), tri, scale, True)

    def body(n, state):
        acc, carry = state
        k0 = pl.multiple_of((qi - 1 - n) * tq, tq)
        out, carry = _sb_block(q, k_ref[pl.ds(k0, tq), :], v_ref[pl.ds(k0, tq), :],
                               carry, tri, scale, False)
        return acc + out, carry

    acc, _ = lax.fori_loop(0, qi, body, (acc, carry))
    o_ref[...] = acc.astype(o_ref.dtype)


def sb_attention(qkv, batch, seq, heads, tq=256):
    m = qkv.shape[0]
    tq = min(tq, seq)
    nq = seq // tq
    kern = functools.partial(_attn_kernel, scale=SB_HEAD_DIM ** -0.5)
    return pl.pallas_call(
        kern,
        grid=(batch, heads, nq),
        in_specs=[
            pl.BlockSpec((tq, SB_HEAD_DIM), lambda b, h, i: (b * nq + i, h)),
            pl.BlockSpec((seq, SB_HEAD_DIM), lambda b, h, i: (b, heads + h)),
            pl.BlockSpec((seq, SB_HEAD_DIM), lambda b, h, i: (b, 2 * heads + h)),
        ],
        out_specs=pl.BlockSpec((tq, SB_HEAD_DIM), lambda b, h, i: (b * nq + i, h)),
        out_shape=jax.ShapeDtypeStruct((m, heads * SB_HEAD_DIM), BF16),
        compiler_params=_params("arbitrary", "arbitrary", "arbitrary"),
        name="sb_attention",
    )(qkv, qkv, qkv)


def _router_kernel(x_ref, g_ref, rt_ref, idx_ref, gate_ref):
    x = x_ref[...]
    ms = jnp.mean(x * x, axis=-1, keepdims=True)
    h = x * lax.rsqrt(ms + EPS) * g_ref[...]
    logits = lax.dot_general(rt_ref[...], h, (((1,), (1,)), ((), ())),
                             precision=lax.Precision.HIGHEST, preferred_element_type=F32)
    n_exp = logits.shape[0]
    eid = lax.broadcasted_iota(jnp.int32, logits.shape, 0)
    m1 = jnp.max(logits, axis=0, keepdims=True)
    i1 = jnp.min(jnp.where(logits == m1, eid, n_exp), axis=0, keepdims=True)
    rest = jnp.where(eid == i1, -jnp.inf, logits)
    m2 = jnp.max(rest, axis=0, keepdims=True)
    i2 = jnp.min(jnp.where(rest == m2, eid, n_exp), axis=0, keepdims=True)
    e2 = jnp.exp(m2 - m1)
    g1 = 1.0 / (1.0 + e2)
    idx_ref[...] = jnp.concatenate([i1, i2], axis=0)
    gate_ref[...] = jnp.concatenate([g1, e2 * g1], axis=0)


def router_top2(x, gain, router, tm=256):
    m, d = x.shape
    n_exp = router.shape[1]
    tm = min(tm, m)
    return pl.pallas_call(
        _router_kernel,
        grid=(m // tm,),
        in_specs=[pl.BlockSpec((tm, d), lambda i: (i, 0)),
                  pl.BlockSpec((1, d), lambda i: (0, 0)),
                  pl.BlockSpec((n_exp, d), lambda i: (0, 0))],
        out_specs=[pl.BlockSpec((TOP_K, tm), lambda i: (0, i)),
                   pl.BlockSpec((TOP_K, tm), lambda i: (0, i))],
        out_shape=[jax.ShapeDtypeStruct((TOP_K, m), jnp.int32),
                   jax.ShapeDtypeStruct((TOP_K, m), F32)],
        compiler_params=_params("arbitrary"),
        name="router_top2",
    )(x, gain.reshape(1, d), router.T)


def _row_copy(src_hbm, src_row, dst_vmem, dst_row, sem):
    return pltpu.make_async_copy(src_hbm.at[pl.ds(src_row, 1), :], dst_vmem.at[pl.ds(dst_row, 1), :], sem)


def _gather_norm_kernel(src_ref, x_hbm, g_ref, o_ref, buf, sem):
    i = pl.program_id(0)
    rows = buf.shape[0]

    def issue(r, c):
        _row_copy(x_hbm, src_ref[i * rows + r], buf, r, sem).start()
        return c

    def wait(r, c):
        _row_copy(x_hbm, 0, buf, r, sem).wait()
        return c

    lax.fori_loop(0, rows, issue, 0)
    lax.fori_loop(0, rows, wait, 0)
    x = buf[...]
    ms = jnp.mean(x * x, axis=-1, keepdims=True)
    o_ref[...] = (x * lax.rsqrt(ms + EPS) * g_ref[...]).astype(o_ref.dtype)


def gather_norm(x, gain, src_rows, rows=128):
    p = src_rows.shape[0]
    d = x.shape[1]
    return pl.pallas_call(
        _gather_norm_kernel,
        grid_spec=pltpu.PrefetchScalarGridSpec(
            num_scalar_prefetch=1,
            grid=(p // rows,),
            in_specs=[pl.BlockSpec(memory_space=pl.ANY),
                      pl.BlockSpec((1, d), lambda i, src: (0, 0))],
            out_specs=pl.BlockSpec((rows, d), lambda i, src: (i, 0)),
            scratch_shapes=[pltpu.VMEM((rows, d), F32), pltpu.SemaphoreType.DMA(())],
        ),
        out_shape=jax.ShapeDtypeStruct((p, d), BF16),
        compiler_params=_params("arbitrary"),
        name="gather_norm",
    )(src_rows, x, gain.reshape(1, d))


def _combine_kernel(pos_ref, x_ref, y_hbm, o_ref, buf, sem):
    i = pl.program_id(0)
    rows = x_ref.shape[0]
    n_tokens = pos_ref.shape[0] // TOP_K

    def issue(r, c):
        for k in range(TOP_K):
            _row_copy(y_hbm, pos_ref[k * n_tokens + i * rows + r], buf.at[k], r, sem).start()
        return c

    def wait(r, c):
        for k in range(TOP_K):
            _row_copy(y_hbm, 0, buf.at[k], r, sem).wait()
        return c

    lax.fori_loop(0, rows, issue, 0)
    lax.fori_loop(0, rows, wait, 0)
    out = x_ref[...]
    for k in range(TOP_K):
        out = out + buf[k]
    o_ref[...] = out


def combine(x, y, pos, rows=128):
    m, d = x.shape
    rows = min(rows, m)
    return pl.pallas_call(
        _combine_kernel,
        grid_spec=pltpu.PrefetchScalarGridSpec(
            num_scalar_prefetch=1,
            grid=(m // rows,),
            in_specs=[pl.BlockSpec((rows, d), lambda i, pos: (i, 0)),
                      pl.BlockSpec(memory_space=pl.ANY)],
            out_specs=pl.BlockSpec((rows, d), lambda i, pos: (i, 0)),
            scratch_shapes=[pltpu.VMEM((TOP_K, rows, d), F32), pltpu.SemaphoreType.DMA(())],
        ),
        out_shape=jax.ShapeDtypeStruct((m, d), F32),
        compiler_params=_params("arbitrary"),
        name="combine",
    )(pos, x, y)


def _routing_tables(idx, gate, n_exp, tm):
    m = idx.shape[1]
    n_assign = TOP_K * m
    n_tiles = n_assign // tm + n_exp
    p = n_tiles * tm
    e_flat = idx.reshape(-1)
    onehot = (e_flat[:, None] == jnp.arange(n_exp)[None, :]).astype(jnp.int32)
    before = jnp.cumsum(onehot, axis=0) - onehot
    rank = jnp.sum(before * onehot, axis=1)
    counts = jnp.sum(onehot, axis=0)
    padded = ((counts + tm - 1) // tm) * tm
    ends = jnp.cumsum(padded)
    offsets = ends - padded
    pos = offsets[e_flat] + rank
    token = jnp.tile(jnp.arange(m, dtype=jnp.int32), TOP_K)
    src_rows = jnp.zeros((p,), jnp.int32).at[pos].set(token)
    gate_sorted = jnp.zeros((p,), F32).at[pos].set(gate.reshape(-1))
    tile_start = jnp.arange(n_tiles, dtype=jnp.int32) * tm
    tile_expert = jnp.sum((tile_start[:, None] >= ends[None, :]).astype(jnp.int32), axis=1)
    tile_valid = (tile_start < ends[-1]).astype(jnp.int32)
    tile_expert = jnp.minimum(tile_expert, n_exp - 1).astype(jnp.int32)
    return src_rows, gate_sorted.reshape(p, 1), pos.astype(jnp.int32), tile_expert, tile_valid


def _even_layer(x, seq, norm1, w_in, pool_w, pool_scale, sg_norm, sg_w, sg_b, w_out, norm2, w1, w3, w2,
                tm=512, tn=512):
    m, d = x.shape
    pool_width = pool_scale.shape[0]
    te, tv = _dense_tiles(m, tm)
    h = rmsnorm(x, norm1)
    z = gmm(h, [w_in[None]], te, tv, tm=tm, tn=tn, out_dtype=F32,
            epilogue=functools.partial(_ep_gelu_cols, pool_width // tn), name="w_in")
    y = mixer(z, pool_w, pool_scale, sg_norm, sg_w, sg_b, seq)
    x = gmm(y, [w_out[None]], te, tv, tm=tm, tn=tn, out_dtype=F32, epilogue=_ep_residual,
            extras=[x], extra_specs=[_tile_spec(tm, tn)], name="e_w_out")
    h = rmsnorm(x, norm2)
    hf = gmm(h, [w1[None], w3[None]], te, tv, tm=tm, tn=tn, out_dtype=BF16, epilogue=_ep_swiglu, name="ffn_up")
    d_ff = w2.shape[0]
    w2h = w2.reshape(2, d_ff // 2, d)
    for half in range(2):
        x = gmm(hf, [w2h], jnp.full_like(te, half), tv, tm=tm, tn=tn, out_dtype=F32, epilogue=_ep_residual,
                extras=[x], extra_specs=[_tile_spec(tm, tn)], x_col_block=half, name="ffn_down")
    return x


def _odd_layer(x, batch, seq, norm1, w_qkv, q_norm, k_norm, w_out, norm2, router, w1, w3, w2,
               tm=512, tn=512):
    m, d = x.shape
    heads = d // SB_HEAD_DIM
    te, tv = _dense_tiles(m, tm)
    h = rmsnorm(x, norm1)
    gain_spec = pl.BlockSpec((1, SB_HEAD_DIM), lambda j, i, te, tv: (0, 0))
    qkv = gmm(h, [w_qkv[None]], te, tv, tm=tm, tn=tn, out_dtype=BF16,
              epilogue=functools.partial(_ep_qkv, d // tn, d // tn),
              extras=[q_norm.reshape(1, -1), k_norm.reshape(1, -1)], extra_specs=[gain_spec, gain_spec],
              name="w_qkv")
    o = sb_attention(qkv, batch, seq, heads)
    x = gmm(o, [w_out[None]], te, tv, tm=tm, tn=tn, out_dtype=F32, epilogue=_ep_residual,
            extras=[x], extra_specs=[_tile_spec(tm, tn)], name="o_w_out")

    n_exp = router.shape[1]
    idx, gate = router_top2(x, norm2, router)
    src_rows, gate_sorted, pos, tile_expert, tile_valid = _routing_tables(idx, gate, n_exp, tm)
    xs = gather_norm(x, norm2, src_rows)
    hs = gmm(xs, [w1, w3], tile_expert, tile_valid, tm=tm, tn=tn, out_dtype=BF16, epilogue=_ep_swiglu,
             name="moe_up")
    ys = gmm(hs, [w2], tile_expert, tile_valid, tm=tm, tn=tn, out_dtype=F32, epilogue=_ep_rowscale,
             extras=[gate_sorted], extra_specs=[pl.BlockSpec((tm, 1), lambda j, i, te, tv: (i, 0))],
             name="moe_down")
    return combine(x, ys, pos)


def kernel(x, e_norm1, e_w_in, e_pool_w, e_pool_scale, e_sg_norm, e_sg_w, e_sg_b, e_w_out, e_norm2, e_ffn_w1, e_ffn_w3, e_ffn_w2, o_norm1, o_w_qkv, o_q_norm, o_k_norm, o_w_out, o_norm2, o_router, o_moe_w1, o_moe_w3, o_moe_w2):
    batch, seq, d = x.shape
    depth = e_norm1.shape[0] + o_norm1.shape[0]
    xf = x.reshape(batch * seq, d)
    for layer in range(depth):
        i = layer // 2
        if layer % 2 == 0:
            xf = _even_layer(xf, seq, e_norm1[i], e_w_in[i], e_pool_w[i], e_pool_scale[i], e_sg_norm[i],
                             e_sg_w[i], e_sg_b[i], e_w_out[i], e_norm2[i], e_ffn_w1[i], e_ffn_w3[i], e_ffn_w2[i])
        else:
            xf = _odd_layer(xf, batch, seq, o_norm1[i], o_w_qkv[i], o_q_norm[i], o_k_norm[i], o_w_out[i],
                            o_norm2[i], o_router[i], o_moe_w1[i], o_moe_w3[i], o_moe_w2[i])
    return xf.reshape(batch, seq, d)
```

```python
import functools

import jax
import jax.numpy as jnp
from jax import lax
from jax.experimental import pallas as pl
from jax.experimental.pallas import tpu as pltpu

EPS = 1e-6
CHUNK = 64
SG_LEN = 128
SG_HEADS = 8
POOL_WINDOWS = (2, 4, 8, 16)
POOL_HALO = 16
SB_HEAD_DIM = 128
SB_EXIT_LOG2 = -150.0
LOG2E = 1.4426950408889634
TOP_K = 2

V7X_VMEM_BYTES = 64 * 1024 * 1024
VMEM_LIMIT_BYTES = V7X_VMEM_BYTES - 8 * 1024 * 1024
LANES = 128

F32 = jnp.float32
BF16 = jnp.bfloat16


def _params(*semantics):
    return pltpu.CompilerParams(dimension_semantics=semantics, vmem_limit_bytes=VMEM_LIMIT_BYTES)


def _rmsnorm_kernel(x_ref, g_ref, o_ref):
    x = x_ref[...]
    ms = jnp.mean(x * x, axis=-1, keepdims=True)
    o_ref[...] = (x * lax.rsqrt(ms + EPS) * g_ref[...]).astype(o_ref.dtype)


def rmsnorm(x, gain, tm=256):
    m, d = x.shape
    tm = min(tm, m)
    return pl.pallas_call(
        _rmsnorm_kernel,
        grid=(m // tm,),
        in_specs=[pl.BlockSpec((tm, d), lambda i: (i, 0)),
                  pl.BlockSpec((1, d), lambda i: (0, 0))],
        out_specs=pl.BlockSpec((tm, d), lambda i: (i, 0)),
        out_shape=jax.ShapeDtypeStruct((m, d), BF16),
        compiler_params=_params("arbitrary"),
        name="rmsnorm",
    )(x, gain.reshape(1, d))


def _gmm_kernel(te_ref, tv_ref, x_ref, *refs, n_w, n_extra, epilogue):
    w_refs = refs[:n_w]
    extra_refs = refs[n_w:n_w + n_extra]
    o_ref = refs[n_w + n_extra]
    wb_refs = refs[n_w + n_extra + 1:]
    j = pl.program_id(0)
    i = pl.program_id(1)

    changed = (i == 0) | (te_ref[i] != te_ref[jnp.maximum(i - 1, 0)])

    @pl.when(changed)
    def _():
        for w, wb in zip(w_refs, wb_refs):
            wb[...] = w[0].astype(BF16)

    @pl.when(tv_ref[i] != 0)
    def _():
        x = x_ref[...]
        accs = [jnp.dot(x, wb[...], preferred_element_type=F32) for wb in wb_refs]
        epilogue(accs, extra_refs, o_ref, j)

    @pl.when(tv_ref[i] == 0)
    def _():
        o_ref[...] = jnp.zeros(o_ref.shape, o_ref.dtype)


def gmm(x, weights, tile_expert, tile_valid, *, tm, tn, epilogue, out_dtype,
        extras=(), extra_specs=(), x_col_block=0, name="gmm"):
    p = x.shape[0]
    _, k, n = weights[0].shape
    grid = (pl.cdiv(n, tn), p // tm)
    x_spec = pl.BlockSpec((tm, k), lambda j, i, te, tv: (i, x_col_block))
    w_spec = pl.BlockSpec((1, k, tn), lambda j, i, te, tv: (te[i], 0, j))
    kern = functools.partial(_gmm_kernel, n_w=len(weights), n_extra=len(extras), epilogue=epilogue)
    return pl.pallas_call(
        kern,
        grid_spec=pltpu.PrefetchScalarGridSpec(
            num_scalar_prefetch=2,
            grid=grid,
            in_specs=[x_spec] + [w_spec] * len(weights) + list(extra_specs),
            out_specs=pl.BlockSpec((tm, tn), lambda j, i, te, tv: (i, j)),
            scratch_shapes=[pltpu.VMEM((k, tn), BF16)] * len(weights),
        ),
        out_shape=jax.ShapeDtypeStruct((p, n), out_dtype),
        compiler_params=_params("arbitrary", "arbitrary"),
        name=name,
    )(tile_expert, tile_valid, x, *weights, *extras)


def _dense_tiles(m, tm):
    n = m // tm
    return jnp.zeros((n,), jnp.int32), jnp.ones((n,), jnp.int32)


def _tile_spec(tm, tn):
    return pl.BlockSpec((tm, tn), lambda j, i, te, tv: (i, j))


def _ep_gelu_cols(first_gelu_block, accs, extra_refs, o_ref, j):
    del extra_refs
    acc = accs[0]

    @pl.when(j < first_gelu_block)
    def _():
        o_ref[...] = acc.astype(o_ref.dtype)

    @pl.when(j >= first_gelu_block)
    def _():
        o_ref[...] = jax.nn.gelu(acc).astype(o_ref.dtype)


def _ep_residual(accs, extra_refs, o_ref, j):
    del j
    o_ref[...] = (extra_refs[0][...] + accs[0]).astype(o_ref.dtype)


def _ep_swiglu(accs, extra_refs, o_ref, j):
    del extra_refs, j
    a, g = accs
    o_ref[...] = (jax.nn.silu(a) * g).astype(o_ref.dtype)


def _ep_plain(accs, extra_refs, o_ref, j):
    del extra_refs, j
    o_ref[...] = accs[0].astype(o_ref.dtype)


def _ep_qkv(q_blocks, k_blocks, accs, extra_refs, o_ref, j):
    acc = accs[0]
    qg_ref, kg_ref = extra_refs
    tn = acc.shape[1]

    @pl.when(j < q_blocks + k_blocks)
    def _():
        gain = jnp.where(j < q_blocks, qg_ref[...], kg_ref[...])
        for c in range(tn // SB_HEAD_DIM):
            sl = slice(c * SB_HEAD_DIM, (c + 1) * SB_HEAD_DIM)
            a = acc[:, sl]
            ms = jnp.mean(a * a, axis=-1, keepdims=True)
            o_ref[:, sl] = (a * lax.rsqrt(ms + EPS) * gain).astype(o_ref.dtype)

    @pl.when(j >= q_blocks + k_blocks)
    def _():
        o_ref[...] = acc.astype(o_ref.dtype)


def _mixer_kernel(za_ref, halo_ref, u_ref, v_ref, pw_ref, ps_ref, sgn_ref, sgw_ref, sgb_ref,
                  o_ref, zbuf, *, tiles_per_seq):
    i = pl.program_id(0)
    ts, pool_width = za_ref.shape
    sg_width = v_ref.shape[1]
    group_width = pool_width // len(POOL_WINDOWS)
    head_dim = sg_width // SG_HEADS
    tile_in_seq = i % tiles_per_seq

    za = za_ref[...]
    zbuf[POOL_HALO:POOL_HALO + ts, :] = za
    zbuf[0:POOL_HALO, :] = jnp.where(tile_in_seq == 0, 0.0, halo_ref[...])
    n_avail = (lax.broadcasted_iota(jnp.int32, (ts, 1), 0) + tile_in_seq * ts + 1).astype(F32)
    for g, w in enumerate(POOL_WINDOWS):
        cols = slice(g * group_width, (g + 1) * group_width)
        win = za[:, cols]
        for back in range(1, w):
            win = win + zbuf[POOL_HALO - back:POOL_HALO - back + ts, cols]
        mean = win / jnp.minimum(n_avail, float(w))
        pooled = (mean - za[:, cols]).astype(BF16)
        ya = jnp.dot(pooled, pw_ref[g], preferred_element_type=F32) * ps_ref[:, cols]
        o_ref[:, cols] = ya.astype(o_ref.dtype)

    v = v_ref[...]
    ms = jnp.mean(v * v, axis=-1, keepdims=True)
    vn = (v * lax.rsqrt(ms + EPS) * sgn_ref[...]).astype(BF16)
    pos_i = lax.broadcasted_iota(jnp.int32, (SG_LEN, SG_LEN), 0)
    pos_j = lax.broadcasted_iota(jnp.int32, (SG_LEN, SG_LEN), 1)
    allowed = (pos_j // CHUNK) <= (pos_i // CHUNK)
    for h in range(SG_HEADS):
        w_h = jnp.where(allowed, sgw_ref[h], 0.0).astype(BF16)
        hc = slice(h * head_dim, (h + 1) * head_dim)
        for c in range(ts // SG_LEN):
            rows = slice(c * SG_LEN, (c + 1) * SG_LEN)
            mixed = jnp.dot(w_h, vn[rows, hc], preferred_element_type=F32) + sgb_ref[:, hc]
            yb = u_ref[rows, hc] * mixed
            o_ref[rows, pool_width + h * head_dim:pool_width + (h + 1) * head_dim] = yb.astype(o_ref.dtype)


def mixer(z, pool_w, pool_scale, sg_norm, sg_w, sg_b, seq, ts=256):
    m = z.shape[0]
    pool_width = pool_scale.shape[0]
    sg_width = sg_norm.shape[0]
    assert pool_width == sg_width
    ts = min(ts, seq)
    tiles_per_seq = seq // ts
    halo_blocks_per_tile = ts // POOL_HALO
    head_dim = sg_width // SG_HEADS
    bias = jnp.repeat(sg_b.T, head_dim, axis=1)
    kern = functools.partial(_mixer_kernel, tiles_per_seq=tiles_per_seq)
    const2 = lambda i: (0, 0)
    const3 = lambda i: (0, 0, 0)
    return pl.pallas_call(
        kern,
        grid=(m // ts,),
        in_specs=[
            pl.BlockSpec((ts, pool_width), lambda i: (i, 0)),
            pl.BlockSpec((POOL_HALO, pool_width), lambda i: (jnp.maximum(i * halo_blocks_per_tile - 1, 0), 0)),
            pl.BlockSpec((ts, sg_width), lambda i: (i, 1)),
            pl.BlockSpec((ts, sg_width), lambda i: (i, 2)),
            pl.BlockSpec(pool_w.shape, const3),
            pl.BlockSpec((1, pool_width), const2),
            pl.BlockSpec((1, sg_width), const2),
            pl.BlockSpec(sg_w.shape, const3),
            pl.BlockSpec(bias.shape, const2),
        ],
        out_specs=pl.BlockSpec((ts, pool_width + sg_width), lambda i: (i, 0)),
        out_shape=jax.ShapeDtypeStruct((m, pool_width + sg_width), BF16),
        scratch_shapes=[pltpu.VMEM((ts + POOL_HALO, pool_width), F32)],
        compiler_params=_params("arbitrary"),
        name="mixer",
    )(z, z, z, z, pool_w.astype(BF16), pool_scale.reshape(1, -1), sg_norm.reshape(1, -1), sg_w, bias)


def _sb_block(q, k_blk, v_blk, carry, neg_tri, strict):
    z = lax.dot_general(q, k_blk, (((1,), (1,)), ((), ())), preferred_element_type=F32)
    neg_abs = lax.bitcast_convert_type(
        lax.bitcast_convert_type(z, jnp.uint32) | jnp.uint32(0x80000000), F32)
    softplus = jnp.maximum(z, 0.0) + jnp.log(1.0 + jnp.exp2(neg_abs)) * LOG2E
    if strict is not None:
        softplus = jnp.where(strict, softplus, 0.0)
    after = jnp.dot(softplus.astype(BF16), neg_tri, preferred_element_type=F32) + carry
    attn = jnp.exp2(z - softplus + after)
    if strict is not None:
        attn = jnp.where(strict, attn, 0.0)
    out = jnp.dot(attn.astype(BF16), v_blk, preferred_element_type=F32)
    return out, carry - jnp.sum(softplus, axis=1, keepdims=True)


def _attn_kernel(q_ref, k_ref, v_ref, o_ref, *, group):
    qi = pl.program_id(2)
    tq = q_ref.shape[0]
    t_pos = lax.broadcasted_iota(jnp.int32, (tq, tq), 0)
    s_pos = lax.broadcasted_iota(jnp.int32, (tq, tq), 1)
    strict = s_pos < t_pos
    neg_tri = jnp.where(strict, -1.0, 0.0).astype(BF16)

    def head(g):
        return slice(g * SB_HEAD_DIM, (g + 1) * SB_HEAD_DIM)

    def sweep(k0, accs, carries, mask):
        new_accs, new_carries = [], []
        for g in range(group):
            out, carry = _sb_block(q_ref[:, head(g)], k_ref[pl.ds(k0, tq), head(g)],
                                   v_ref[pl.ds(k0, tq), head(g)], carries[g], neg_tri, mask)
            new_accs.append(out if accs is None else accs[g] + out)
            new_carries.append(carry)
        return tuple(new_accs), tuple(new_carries)

    def alive(carries):
        top = functools.reduce(jnp.maximum, carries)
        return (jnp.max(top) > SB_EXIT_LOG2).astype(jnp.int32)

    zero = jnp.zeros((tq, 1), F32)
    accs, carries = sweep(pl.multiple_of(qi * tq, tq), None, (zero,) * group, strict)

    def cond(state):
        n, live = state[0], state[1]
        return jnp.logical_and(n < qi, live > 0)

    def body(state):
        n, _, accs, carries = state
        accs, carries = sweep(pl.multiple_of((qi - 1 - n) * tq, tq), accs, carries, None)
        return n + 1, alive(carries), accs, carries

    _, _, accs, _ = lax.while_loop(cond, body, (jnp.int32(0), alive(carries), accs, carries))
    for g in range(group):
        o_ref[:, head(g)] = accs[g].astype(o_ref.dtype)


def sb_attention(qkv, batch, seq, heads, tq=256, group=8):
    m = qkv.shape[0]
    tq = min(tq, seq)
    group = min(group, heads)
    nq = seq // tq
    n_groups = heads // group
    width = group * SB_HEAD_DIM
    kern = functools.partial(_attn_kernel, group=group)
    return pl.pallas_call(
        kern,
        grid=(batch, n_groups, nq),
        in_specs=[
            pl.BlockSpec((tq, width), lambda b, h, i: (b * nq + i, h)),
            pl.BlockSpec((seq, width), lambda b, h, i: (b, n_groups + h)),
            pl.BlockSpec((seq, width), lambda b, h, i: (b, 2 * n_groups + h)),
        ],
        out_specs=pl.BlockSpec((tq, width), lambda b, h, i: (b * nq + i, h)),
        out_shape=jax.ShapeDtypeStruct((m, heads * SB_HEAD_DIM), BF16),
        compiler_params=_params("arbitrary", "arbitrary", "arbitrary"),
        name="sb_attention",
    )(qkv, qkv, qkv)


def _router_kernel(x_ref, g_ref, rt_ref, idx_ref, gate_ref):
    x = x_ref[...]
    ms = jnp.mean(x * x, axis=-1, keepdims=True)
    h = x * lax.rsqrt(ms + EPS) * g_ref[...]
    logits = lax.dot_general(rt_ref[...], h, (((1,), (1,)), ((), ())),
                             precision=lax.Precision.HIGHEST, preferred_element_type=F32)
    n_exp = logits.shape[0]
    eid = lax.broadcasted_iota(jnp.int32, logits.shape, 0)
    m1 = jnp.max(logits, axis=0, keepdims=True)
    i1 = jnp.min(jnp.where(logits == m1, eid, n_exp), axis=0, keepdims=True)
    rest = jnp.where(eid == i1, -jnp.inf, logits)
    m2 = jnp.max(rest, axis=0, keepdims=True)
    i2 = jnp.min(jnp.where(rest == m2, eid, n_exp), axis=0, keepdims=True)
    e2 = jnp.exp(m2 - m1)
    g1 = 1.0 / (1.0 + e2)
    idx_ref[...] = jnp.concatenate([i1, i2], axis=0)
    gate_ref[...] = jnp.concatenate([g1, e2 * g1], axis=0)


def router_top2(x, gain, router, tm=256):
    m, d = x.shape
    n_exp = router.shape[1]
    tm = min(tm, m)
    return pl.pallas_call(
        _router_kernel,
        grid=(m // tm,),
        in_specs=[pl.BlockSpec((tm, d), lambda i: (i, 0)),
                  pl.BlockSpec((1, d), lambda i: (0, 0)),
                  pl.BlockSpec((n_exp, d), lambda i: (0, 0))],
        out_specs=[pl.BlockSpec((TOP_K, tm), lambda i: (0, i)),
                   pl.BlockSpec((TOP_K, tm), lambda i: (0, i))],
        out_shape=[jax.ShapeDtypeStruct((TOP_K, m), jnp.int32),
                   jax.ShapeDtypeStruct((TOP_K, m), F32)],
        compiler_params=_params("arbitrary"),
        name="router_top2",
    )(x, gain.reshape(1, d), router.T)


def _row_copy(src_hbm, src_row, dst_vmem, dst_row, sem):
    return pltpu.make_async_copy(src_hbm.at[pl.ds(src_row, 1), :], dst_vmem.at[pl.ds(dst_row, 1), :], sem)


def _issue_rows(src_hbm, index_of_row, dst_vmem, sem):
    def one(r, c):
        _row_copy(src_hbm, index_of_row(r), dst_vmem, r, sem).start()
        return c
    lax.fori_loop(0, dst_vmem.shape[0], one, 0, unroll=8)


def _wait_rows(src_hbm, dst_vmem, sem):
    def one(r, c):
        _row_copy(src_hbm, 0, dst_vmem, r, sem).wait()
        return c
    lax.fori_loop(0, dst_vmem.shape[0], one, 0, unroll=8)


def _gather_norm_kernel(src_ref, x_hbm, g_ref, o_ref, buf, sems):
    i = pl.program_id(0)
    rows = o_ref.shape[0]
    slot = i % 2

    def issue(tile, slot):
        _issue_rows(x_hbm, lambda r: src_ref[tile * rows + r], buf.at[slot], sems.at[slot])

    @pl.when(i == 0)
    def _():
        issue(0, 0)

    @pl.when(i + 1 < pl.num_programs(0))
    def _():
        issue(i + 1, 1 - slot)

    _wait_rows(x_hbm, buf.at[slot], sems.at[slot])
    x = buf[slot]
    ms = jnp.mean(x * x, axis=-1, keepdims=True)
    o_ref[...] = (x * lax.rsqrt(ms + EPS) * g_ref[...]).astype(o_ref.dtype)


def gather_norm(x, gain, src_rows, rows=256):
    p = src_rows.shape[0]
    d = x.shape[1]
    return pl.pallas_call(
        _gather_norm_kernel,
        grid_spec=pltpu.PrefetchScalarGridSpec(
            num_scalar_prefetch=1,
            grid=(p // rows,),
            in_specs=[pl.BlockSpec(memory_space=pl.ANY),
                      pl.BlockSpec((1, d), lambda i, src: (0, 0))],
            out_specs=pl.BlockSpec((rows, d), lambda i, src: (i, 0)),
            scratch_shapes=[pltpu.VMEM((2, rows, d), F32), pltpu.SemaphoreType.DMA((2,))],
        ),
        out_shape=jax.ShapeDtypeStruct((p, d), BF16),
        compiler_params=_params("arbitrary"),
        name="gather_norm",
    )(src_rows, x, gain.reshape(1, d))


def _combine_kernel(pos_ref, x_ref, gate_ref, y_hbm, o_ref, buf, sems):
    i = pl.program_id(0)
    rows = x_ref.shape[0]
    n_tokens = pos_ref.shape[0] // TOP_K
    slot = i % 2

    def issue(tile, slot):
        for k in range(TOP_K):
            _issue_rows(y_hbm, lambda r: pos_ref[k * n_tokens + tile * rows + r],
                        buf.at[slot, k], sems.at[slot])

    @pl.when(i == 0)
    def _():
        issue(0, 0)

    @pl.when(i + 1 < pl.num_programs(0))
    def _():
        issue(i + 1, 1 - slot)

    for k in range(TOP_K):
        _wait_rows(y_hbm, buf.at[slot, k], sems.at[slot])
    out = x_ref[...]
    for k in range(TOP_K):
        out = out + gate_ref[:, k:k + 1] * buf[slot, k]
    o_ref[...] = out


def combine(x, y, pos, gate, rows=128):
    m, d = x.shape
    rows = min(rows, m)
    return pl.pallas_call(
        _combine_kernel,
        grid_spec=pltpu.PrefetchScalarGridSpec(
            num_scalar_prefetch=1,
            grid=(m // rows,),
            in_specs=[pl.BlockSpec((rows, d), lambda i, pos: (i, 0)),
                      pl.BlockSpec((rows, TOP_K), lambda i, pos: (i, 0)),
                      pl.BlockSpec(memory_space=pl.ANY)],
            out_specs=pl.BlockSpec((rows, d), lambda i, pos: (i, 0)),
            scratch_shapes=[pltpu.VMEM((2, TOP_K, rows, d), F32), pltpu.SemaphoreType.DMA((2,))],
        ),
        out_shape=jax.ShapeDtypeStruct((m, d), F32),
        compiler_params=_params("arbitrary"),
        name="combine",
    )(pos, x, gate.T, y)


def _routing_tables(idx, n_exp, tm):
    m = idx.shape[1]
    n_assign = TOP_K * m
    n_tiles = n_assign // tm + n_exp
    p = n_tiles * tm
    e_flat = idx.reshape(-1)
    onehot = (e_flat[:, None] == jnp.arange(n_exp)[None, :]).astype(jnp.int32)
    before = jnp.cumsum(onehot, axis=0) - onehot
    rank = jnp.sum(before * onehot, axis=1)
    counts = jnp.sum(onehot, axis=0)
    padded = ((counts + tm - 1) // tm) * tm
    ends = jnp.cumsum(padded)
    offsets = ends - padded
    pos = offsets[e_flat] + rank
    token = jnp.tile(jnp.arange(m, dtype=jnp.int32), TOP_K)
    src_rows = jnp.zeros((p,), jnp.int32).at[pos].set(token)
    tile_start = jnp.arange(n_tiles, dtype=jnp.int32) * tm
    tile_expert = jnp.sum((tile_start[:, None] >= ends[None, :]).astype(jnp.int32), axis=1)
    tile_valid = (tile_start < ends[-1]).astype(jnp.int32)
    tile_expert = jnp.minimum(tile_expert, n_exp - 1).astype(jnp.int32)
    return src_rows, pos.astype(jnp.int32), tile_expert, tile_valid


def _even_layer(x, seq, norm1, w_in, pool_w, pool_scale, sg_norm, sg_w, sg_b, w_out, norm2, w1, w3, w2,
                tm=512, tm_wide=1024, tn=512):
    m, d = x.shape
    pool_width = pool_scale.shape[0]
    tm_wide = min(tm_wide, m)
    te, tv = _dense_tiles(m, tm)
    te_wide, tv_wide = _dense_tiles(m, tm_wide)
    h = rmsnorm(x, norm1)
    z = gmm(h, [w_in[None]], te_wide, tv_wide, tm=tm_wide, tn=tn, out_dtype=F32,
            epilogue=functools.partial(_ep_gelu_cols, pool_width // tn), name="w_in")
    y = mixer(z, pool_w, pool_scale, sg_norm, sg_w, sg_b, seq)
    x = gmm(y, [w_out[None]], te_wide, tv_wide, tm=tm_wide, tn=tn, out_dtype=F32, epilogue=_ep_residual,
            extras=[x], extra_specs=[_tile_spec(tm_wide, tn)], name="e_w_out")
    h = rmsnorm(x, norm2)
    hf = gmm(h, [w1[None], w3[None]], te, tv, tm=tm, tn=tn, out_dtype=BF16, epilogue=_ep_swiglu, name="ffn_up")
    d_ff = w2.shape[0]
    w2h = w2.reshape(2, d_ff // 2, d)
    for half in range(2):
        x = gmm(hf, [w2h], jnp.full_like(te, half), tv, tm=tm, tn=tn, out_dtype=F32, epilogue=_ep_residual,
                extras=[x], extra_specs=[_tile_spec(tm, tn)], x_col_block=half, name="ffn_down")
    return x


def _odd_layer(x, batch, seq, norm1, w_qkv, q_norm, k_norm, w_out, norm2, router, w1, w3, w2,
               tm=512, tm_wide=1024, tn=512):
    m, d = x.shape
    heads = d // SB_HEAD_DIM
    tm_wide = min(tm_wide, m)
    te_wide, tv_wide = _dense_tiles(m, tm_wide)
    h = rmsnorm(x, norm1)
    gain_spec = pl.BlockSpec((1, SB_HEAD_DIM), lambda j, i, te, tv: (0, 0))
    q_gain = q_norm * (SB_HEAD_DIM ** -0.5 * LOG2E)
    qkv = gmm(h, [w_qkv[None]], te_wide, tv_wide, tm=tm_wide, tn=tn, out_dtype=BF16,
              epilogue=functools.partial(_ep_qkv, d // tn, d // tn),
              extras=[q_gain.reshape(1, -1), k_norm.reshape(1, -1)], extra_specs=[gain_spec, gain_spec],
              name="w_qkv")
    o = sb_attention(qkv, batch, seq, heads)
    x = gmm(o, [w_out[None]], te_wide, tv_wide, tm=tm_wide, tn=tn, out_dtype=F32, epilogue=_ep_residual,
            extras=[x], extra_specs=[_tile_spec(tm_wide, tn)], name="o_w_out")

    n_exp = router.shape[1]
    idx, gate = router_top2(x, norm2, router)
    src_rows, pos, tile_expert, tile_valid = _routing_tables(idx, n_exp, tm)
    xs = gather_norm(x, norm2, src_rows)
    hs = gmm(xs, [w1, w3], tile_expert, tile_valid, tm=tm, tn=tn, out_dtype=BF16, epilogue=_ep_swiglu,
             name="moe_up")
    ys = gmm(hs, [w2], tile_expert, tile_valid, tm=tm, tn=tn, out_dtype=F32, epilogue=_ep_plain,
             name="moe_down")
    return combine(x, ys, pos, gate)


def kernel(x, e_norm1, e_w_in, e_pool_w, e_pool_scale, e_sg_norm, e_sg_w, e_sg_b, e_w_out, e_norm2, e_ffn_w1, e_ffn_w3, e_ffn_w2, o_norm1, o_w_qkv, o_q_norm, o_k_norm, o_w_out, o_norm2, o_router, o_moe_w1, o_moe_w3, o_moe_w2):
    batch, seq, d = x.shape
    depth = e_norm1.shape[0] + o_norm1.shape[0]
    xf = x.reshape(batch * seq, d)
    for layer in range(depth):
        i = layer // 2
        if layer % 2 == 0:
            xf = _even_layer(xf, seq, e_norm1[i], e_w_in[i], e_pool_w[i], e_pool_scale[i], e_sg_norm[i],
                             e_sg_w[i], e_sg_b[i], e_w_out[i], e_norm2[i], e_ffn_w1[i], e_ffn_w3[i], e_ffn_w2[i])
        else:
            xf = _odd_layer(xf, batch, seq, o_norm1[i], o_w_qkv[i], o_q_norm[i], o_k_norm[i], o_w_out[i],
                            o_norm2[i], o_router[i], o_moe_w1[i], o_moe_w3[i], o_moe_w2[i])
    return xf.reshape(batch, seq, d)
```

```python
import functools

import jax
import jax.numpy as jnp
from jax import lax
from jax.experimental import pallas as pl
from jax.experimental.pallas import tpu as pltpu

EPS = 1e-6
CHUNK = 64
SG_LEN = 128
SG_HEADS = 8
POOL_WINDOWS = (2, 4, 8, 16)
POOL_HALO = 16
SB_HEAD_DIM = 128
SB_EXIT_LOG2 = -150.0
LOG2E = 1.4426950408889634
TOP_K = 2

V7X_VMEM_BYTES = 64 * 1024 * 1024
VMEM_LIMIT_BYTES = V7X_VMEM_BYTES - 8 * 1024 * 1024
LANES = 128

F32 = jnp.float32
BF16 = jnp.bfloat16


def _params(*semantics):
    return pltpu.CompilerParams(dimension_semantics=semantics, vmem_limit_bytes=VMEM_LIMIT_BYTES)


def _rmsnorm_kernel(x_ref, g_ref, o_ref):
    x = x_ref[...]
    ms = jnp.mean(x * x, axis=-1, keepdims=True)
    o_ref[...] = (x * lax.rsqrt(ms + EPS) * g_ref[...]).astype(o_ref.dtype)


def rmsnorm(x, gain, tm=256):
    m, d = x.shape
    tm = min(tm, m)
    return pl.pallas_call(
        _rmsnorm_kernel,
        grid=(m // tm,),
        in_specs=[pl.BlockSpec((tm, d), lambda i: (i, 0)),
                  pl.BlockSpec((1, d), lambda i: (0, 0))],
        out_specs=pl.BlockSpec((tm, d), lambda i: (i, 0)),
        out_shape=jax.ShapeDtypeStruct((m, d), BF16),
        compiler_params=_params("arbitrary"),
        name="rmsnorm",
    )(x, gain.reshape(1, d))


def _gmm_kernel(te_ref, tv_ref, x_ref, *refs, n_w, n_extra, epilogue):
    del te_ref
    w_refs = refs[:n_w]
    extra_refs = refs[n_w:n_w + n_extra]
    o_ref = refs[n_w + n_extra]
    j = pl.program_id(0)
    i = pl.program_id(1)

    @pl.when(tv_ref[i] != 0)
    def _():
        x = x_ref[...]
        accs = [jnp.dot(x, w[0].astype(BF16), preferred_element_type=F32) for w in w_refs]
        epilogue(accs, extra_refs, o_ref, j)

    @pl.when(tv_ref[i] == 0)
    def _():
        o_ref[...] = jnp.zeros(o_ref.shape, o_ref.dtype)


def gmm(x, weights, tile_expert, tile_valid, *, tm, tn, epilogue, out_dtype,
        extras=(), extra_specs=(), x_col_block=0, name="gmm"):
    p = x.shape[0]
    _, k, n = weights[0].shape
    grid = (pl.cdiv(n, tn), p // tm)
    x_spec = pl.BlockSpec((tm, k), lambda j, i, te, tv: (i, x_col_block))
    w_spec = pl.BlockSpec((1, k, tn), lambda j, i, te, tv: (te[i], 0, j))
    kern = functools.partial(_gmm_kernel, n_w=len(weights), n_extra=len(extras), epilogue=epilogue)
    return pl.pallas_call(
        kern,
        grid_spec=pltpu.PrefetchScalarGridSpec(
            num_scalar_prefetch=2,
            grid=grid,
            in_specs=[x_spec] + [w_spec] * len(weights) + list(extra_specs),
            out_specs=pl.BlockSpec((tm, tn), lambda j, i, te, tv: (i, j)),
        ),
        out_shape=jax.ShapeDtypeStruct((p, n), out_dtype),
        compiler_params=_params("arbitrary", "arbitrary"),
        name=name,
    )(tile_expert, tile_valid, x, *weights, *extras)


def _dense_tiles(m, tm):
    n = m // tm
    return jnp.zeros((n,), jnp.int32), jnp.ones((n,), jnp.int32)


def _tile_spec(tm, tn):
    return pl.BlockSpec((tm, tn), lambda j, i, te, tv: (i, j))


def _ep_gelu_cols(first_gelu_block, accs, extra_refs, o_ref, j):
    del extra_refs
    acc = accs[0]

    @pl.when(j < first_gelu_block)
    def _():
        o_ref[...] = acc.astype(o_ref.dtype)

    @pl.when(j >= first_gelu_block)
    def _():
        o_ref[...] = jax.nn.gelu(acc).astype(o_ref.dtype)


def _ep_residual(accs, extra_refs, o_ref, j):
    del j
    o_ref[...] = (extra_refs[0][...] + accs[0]).astype(o_ref.dtype)


def _ep_swiglu(accs, extra_refs, o_ref, j):
    del extra_refs, j
    a, g = accs
    o_ref[...] = (jax.nn.silu(a) * g).astype(o_ref.dtype)


def _ep_plain(accs, extra_refs, o_ref, j):
    del extra_refs, j
    o_ref[...] = accs[0].astype(o_ref.dtype)


def _ep_qkv(q_blocks, k_blocks, accs, extra_refs, o_ref, j):
    acc = accs[0]
    qg_ref, kg_ref = extra_refs
    tn = acc.shape[1]

    @pl.when(j < q_blocks + k_blocks)
    def _():
        gain = jnp.where(j < q_blocks, qg_ref[...], kg_ref[...])
        for c in range(tn // SB_HEAD_DIM):
            sl = slice(c * SB_HEAD_DIM, (c + 1) * SB_HEAD_DIM)
            a = acc[:, sl]
            ms = jnp.mean(a * a, axis=-1, keepdims=True)
            o_ref[:, sl] = (a * lax.rsqrt(ms + EPS) * gain).astype(o_ref.dtype)

    @pl.when(j >= q_blocks + k_blocks)
    def _():
        o_ref[...] = acc.astype(o_ref.dtype)


def _mixer_kernel(za_ref, halo_ref, u_ref, v_ref, pw_ref, ps_ref, sgn_ref, sgw_ref, sgb_ref,
                  o_ref, zbuf, *, tiles_per_seq):
    i = pl.program_id(0)
    ts, pool_width = za_ref.shape
    sg_width = v_ref.shape[1]
    group_width = pool_width // len(POOL_WINDOWS)
    head_dim = sg_width // SG_HEADS
    tile_in_seq = i % tiles_per_seq

    za = za_ref[...]
    zbuf[POOL_HALO:POOL_HALO + ts, :] = za
    zbuf[0:POOL_HALO, :] = jnp.where(tile_in_seq == 0, 0.0, halo_ref[...])
    n_avail = (lax.broadcasted_iota(jnp.int32, (ts, 1), 0) + tile_in_seq * ts + 1).astype(F32)
    for g, w in enumerate(POOL_WINDOWS):
        cols = slice(g * group_width, (g + 1) * group_width)
        win = za[:, cols]
        for back in range(1, w):
            win = win + zbuf[POOL_HALO - back:POOL_HALO - back + ts, cols]
        mean = win / jnp.minimum(n_avail, float(w))
        pooled = (mean - za[:, cols]).astype(BF16)
        ya = jnp.dot(pooled, pw_ref[g], preferred_element_type=F32) * ps_ref[:, cols]
        o_ref[:, cols] = ya.astype(o_ref.dtype)

    v = v_ref[...]
    ms = jnp.mean(v * v, axis=-1, keepdims=True)
    vn = (v * lax.rsqrt(ms + EPS) * sgn_ref[...]).astype(BF16)
    pos_i = lax.broadcasted_iota(jnp.int32, (SG_LEN, SG_LEN), 0)
    pos_j = lax.broadcasted_iota(jnp.int32, (SG_LEN, SG_LEN), 1)
    allowed = (pos_j // CHUNK) <= (pos_i // CHUNK)
    for h in range(SG_HEADS):
        w_h = jnp.where(allowed, sgw_ref[h], 0.0).astype(BF16)
        hc = slice(h * head_dim, (h + 1) * head_dim)
        for c in range(ts // SG_LEN):
            rows = slice(c * SG_LEN, (c + 1) * SG_LEN)
            mixed = jnp.dot(w_h, vn[rows, hc], preferred_element_type=F32) + sgb_ref[:, hc]
            yb = u_ref[rows, hc] * mixed
            o_ref[rows, pool_width + h * head_dim:pool_width + (h + 1) * head_dim] = yb.astype(o_ref.dtype)


def mixer(z, pool_w, pool_scale, sg_norm, sg_w, sg_b, seq, ts=256):
    m = z.shape[0]
    pool_width = pool_scale.shape[0]
    sg_width = sg_norm.shape[0]
    assert pool_width == sg_width
    ts = min(ts, seq)
    tiles_per_seq = seq // ts
    halo_blocks_per_tile = ts // POOL_HALO
    head_dim = sg_width // SG_HEADS
    bias = jnp.repeat(sg_b.T, head_dim, axis=1)
    kern = functools.partial(_mixer_kernel, tiles_per_seq=tiles_per_seq)
    const2 = lambda i: (0, 0)
    const3 = lambda i: (0, 0, 0)
    return pl.pallas_call(
        kern,
        grid=(m // ts,),
        in_specs=[
            pl.BlockSpec((ts, pool_width), lambda i: (i, 0)),
            pl.BlockSpec((POOL_HALO, pool_width), lambda i: (jnp.maximum(i * halo_blocks_per_tile - 1, 0), 0)),
            pl.BlockSpec((ts, sg_width), lambda i: (i, 1)),
            pl.BlockSpec((ts, sg_width), lambda i: (i, 2)),
            pl.BlockSpec(pool_w.shape, const3),
            pl.BlockSpec((1, pool_width), const2),
            pl.BlockSpec((1, sg_width), const2),
            pl.BlockSpec(sg_w.shape, const3),
            pl.BlockSpec(bias.shape, const2),
        ],
        out_specs=pl.BlockSpec((ts, pool_width + sg_width), lambda i: (i, 0)),
        out_shape=jax.ShapeDtypeStruct((m, pool_width + sg_width), BF16),
        scratch_shapes=[pltpu.VMEM((ts + POOL_HALO, pool_width), F32)],
        compiler_params=_params("arbitrary"),
        name="mixer",
    )(z, z, z, z, pool_w.astype(BF16), pool_scale.reshape(1, -1), sg_norm.reshape(1, -1), sg_w, bias)


def _sb_block(q, k_blk, v_blk, carry, neg_tri, strict):
    z = lax.dot_general(q, k_blk, (((1,), (1,)), ((), ())), preferred_element_type=F32)
    neg_abs = lax.bitcast_convert_type(
        lax.bitcast_convert_type(z, jnp.uint32) | jnp.uint32(0x80000000), F32)
    softplus = jnp.maximum(z, 0.0) + jnp.log(1.0 + jnp.exp2(neg_abs)) * LOG2E
    if strict is not None:
        softplus = jnp.where(strict, softplus, 0.0)
    after = jnp.dot(softplus.astype(BF16), neg_tri, preferred_element_type=F32) + carry
    attn = jnp.exp2(z - softplus + after)
    if strict is not None:
        attn = jnp.where(strict, attn, 0.0)
    out = jnp.dot(attn.astype(BF16), v_blk, preferred_element_type=F32)
    return out, carry - jnp.sum(softplus, axis=1, keepdims=True)


def _attn_kernel(q_ref, k_ref, v_ref, o_ref, *, group):
    qi = pl.program_id(2)
    tq = q_ref.shape[0]
    t_pos = lax.broadcasted_iota(jnp.int32, (tq, tq), 0)
    s_pos = lax.broadcasted_iota(jnp.int32, (tq, tq), 1)
    strict = s_pos < t_pos
    neg_tri = jnp.where(strict, -1.0, 0.0).astype(BF16)

    def head(g):
        return slice(g * SB_HEAD_DIM, (g + 1) * SB_HEAD_DIM)

    def sweep(k0, accs, carries, mask):
        new_accs, new_carries = [], []
        for g in range(group):
            out, carry = _sb_block(q_ref[:, head(g)], k_ref[pl.ds(k0, tq), head(g)],
                                   v_ref[pl.ds(k0, tq), head(g)], carries[g], neg_tri, mask)
            new_accs.append(out if accs is None else accs[g] + out)
            new_carries.append(carry)
        return tuple(new_accs), tuple(new_carries)

    def alive(carries):
        top = functools.reduce(jnp.maximum, carries)
        return (jnp.max(top) > SB_EXIT_LOG2).astype(jnp.int32)

    zero = jnp.zeros((tq, 1), F32)
    accs, carries = sweep(pl.multiple_of(qi * tq, tq), None, (zero,) * group, strict)

    def cond(state):
        n, live = state[0], state[1]
        return jnp.logical_and(n < qi, live > 0)

    def body(state):
        n, _, accs, carries = state
        accs, carries = sweep(pl.multiple_of((qi - 1 - n) * tq, tq), accs, carries, None)
        return n + 1, alive(carries), accs, carries

    _, _, accs, _ = lax.while_loop(cond, body, (jnp.int32(0), alive(carries), accs, carries))
    for g in range(group):
        o_ref[:, head(g)] = accs[g].astype(o_ref.dtype)


def sb_attention(qkv, batch, seq, heads, tq=256, group=8):
    m = qkv.shape[0]
    tq = min(tq, seq)
    group = min(group, heads)
    nq = seq // tq
    n_groups = heads // group
    width = group * SB_HEAD_DIM
    kern = functools.partial(_attn_kernel, group=group)
    return pl.pallas_call(
        kern,
        grid=(batch, n_groups, nq),
        in_specs=[
            pl.BlockSpec((tq, width), lambda b, h, i: (b * nq + i, h)),
            pl.BlockSpec((seq, width), lambda b, h, i: (b, n_groups + h)),
            pl.BlockSpec((seq, width), lambda b, h, i: (b, 2 * n_groups + h)),
        ],
        out_specs=pl.BlockSpec((tq, width), lambda b, h, i: (b * nq + i, h)),
        out_shape=jax.ShapeDtypeStruct((m, heads * SB_HEAD_DIM), BF16),
        compiler_params=_params("arbitrary", "arbitrary", "arbitrary"),
        name="sb_attention",
    )(qkv, qkv, qkv)


def _router_kernel(x_ref, g_ref, rt_ref, idx_ref, gate_ref, hp_ref):
    x = x_ref[...]
    tm, d = x.shape
    ms = jnp.mean(x * x, axis=-1, keepdims=True)
    h = x * lax.rsqrt(ms + EPS) * g_ref[...]
    logits = lax.dot_general(rt_ref[...], h, (((1,), (1,)), ((), ())),
                             precision=lax.Precision.HIGHEST, preferred_element_type=F32)
    n_exp = logits.shape[0]
    eid = lax.broadcasted_iota(jnp.int32, logits.shape, 0)
    m1 = jnp.max(logits, axis=0, keepdims=True)
    i1 = jnp.min(jnp.where(logits == m1, eid, n_exp), axis=0, keepdims=True)
    rest = jnp.where(eid == i1, -jnp.inf, logits)
    m2 = jnp.max(rest, axis=0, keepdims=True)
    i2 = jnp.min(jnp.where(rest == m2, eid, n_exp), axis=0, keepdims=True)
    e2 = jnp.exp(m2 - m1)
    g1 = 1.0 / (1.0 + e2)
    idx_ref[...] = jnp.concatenate([i1, i2], axis=0)
    gate_ref[...] = jnp.concatenate([g1, e2 * g1], axis=0)

    half = d // 2
    chunks = half // LANES
    lo = lax.bitcast_convert_type(h[:, :half].astype(BF16).astype(F32), jnp.uint32)
    hi = lax.bitcast_convert_type(h[:, half:].astype(BF16).astype(F32), jnp.uint32)
    words = hi | (lo >> 16)
    for c in range(chunks):
        hp_ref[pl.ds(c, tm, stride=chunks), :] = words[:, c * LANES:(c + 1) * LANES]


def router_top2(x, gain, router, tm=256):
    m, d = x.shape
    n_exp = router.shape[1]
    tm = min(tm, m)
    chunks = d // 2 // LANES
    return pl.pallas_call(
        _router_kernel,
        grid=(m // tm,),
        in_specs=[pl.BlockSpec((tm, d), lambda i: (i, 0)),
                  pl.BlockSpec((1, d), lambda i: (0, 0)),
                  pl.BlockSpec((n_exp, d), lambda i: (0, 0))],
        out_specs=[pl.BlockSpec((TOP_K, tm), lambda i: (0, i)),
                   pl.BlockSpec((TOP_K, tm), lambda i: (0, i)),
                   pl.BlockSpec((tm * chunks, LANES), lambda i: (i, 0))],
        out_shape=[jax.ShapeDtypeStruct((TOP_K, m), jnp.int32),
                   jax.ShapeDtypeStruct((TOP_K, m), F32),
                   jax.ShapeDtypeStruct((m * chunks, LANES), jnp.uint32)],
        compiler_params=_params("arbitrary"),
        name="router_top2",
    )(x, gain.reshape(1, d), router.T)


ISSUE_UNROLL = 8


def _row_start(row, height):
    if height == 1 or isinstance(row, int):
        return row * height
    return pl.multiple_of(row * height, height)


def _row_copy(src_hbm, src_row, dst_vmem, dst_row, sem, height):
    return pltpu.make_async_copy(src_hbm.at[pl.ds(_row_start(src_row, height), height), :],
                                 dst_vmem.at[pl.ds(_row_start(dst_row, height), height), :], sem)


def _issue_rows(src_hbm, index_of_row, dst_vmem, sem, height=1):
    def group(b, c):
        for u in range(ISSUE_UNROLL):
            r = b * ISSUE_UNROLL + u
            _row_copy(src_hbm, index_of_row(r), dst_vmem, r, sem, height).start(priority=u % 2)
        return c
    lax.fori_loop(0, dst_vmem.shape[0] // height // ISSUE_UNROLL, group, 0)


def _wait_rows(src_hbm, dst_vmem, sem, height=1):
    def one(r, c):
        _row_copy(src_hbm, 0, dst_vmem, r, sem, height).wait()
        return c
    lax.fori_loop(0, dst_vmem.shape[0] // height, one, 0, unroll=ISSUE_UNROLL)


def _unpack_rows(x_ref, tm, chunks):
    words = jnp.concatenate([x_ref[pl.ds(c, tm, stride=chunks), :] for c in range(chunks)], axis=1)
    lo = lax.bitcast_convert_type(words << 16, F32).astype(BF16)
    hi = lax.bitcast_convert_type(words & jnp.uint32(0xFFFF0000), F32).astype(BF16)
    return jnp.concatenate([lo, hi], axis=1)


def _gather_kernel(src_ref, x_hbm, o_ref, buf, sems, *, height):
    i = pl.program_id(0)
    rows = o_ref.shape[0]
    slot = i % 2

    def issue(tile, slot):
        _issue_rows(x_hbm, lambda r: src_ref[tile * rows + r], buf.at[slot], sems.at[slot], height)

    @pl.when(i == 0)
    def _():
        issue(0, 0)

    @pl.when(i + 1 < pl.num_programs(0))
    def _():
        issue(i + 1, 1 - slot)

    _wait_rows(x_hbm, buf.at[slot], sems.at[slot], height)
    o_ref[...] = _unpack_rows(buf.at[slot], rows, height)


def gather_rows(x, src_rows, height, rows=256):
    p = src_rows.shape[0]
    kern = functools.partial(_gather_kernel, height=height)
    return pl.pallas_call(
        kern,
        grid_spec=pltpu.PrefetchScalarGridSpec(
            num_scalar_prefetch=1,
            grid=(p // rows,),
            in_specs=[pl.BlockSpec(memory_space=pl.ANY)],
            out_specs=pl.BlockSpec((rows, 2 * LANES * height), lambda i, src: (i, 0)),
            scratch_shapes=[pltpu.VMEM((2, rows * height, LANES), x.dtype), pltpu.SemaphoreType.DMA((2,))],
        ),
        out_shape=jax.ShapeDtypeStruct((p, 2 * LANES * height), BF16),
        compiler_params=_params("arbitrary"),
        name="gather_rows",
    )(src_rows, x)


def _combine_kernel(pos_ref, x_ref, gate_ref, y_hbm, o_ref, buf, sems):
    i = pl.program_id(0)
    rows = x_ref.shape[0]
    n_tokens = pos_ref.shape[0] // TOP_K
    slot = i % 2

    def issue(tile, slot):
        for k in range(TOP_K):
            _issue_rows(y_hbm, lambda r: pos_ref[k * n_tokens + tile * rows + r],
                        buf.at[slot, k], sems.at[slot])

    @pl.when(i == 0)
    def _():
        issue(0, 0)

    @pl.when(i + 1 < pl.num_programs(0))
    def _():
        issue(i + 1, 1 - slot)

    for k in range(TOP_K):
        _wait_rows(y_hbm, buf.at[slot, k], sems.at[slot])
    out = x_ref[...]
    for k in range(TOP_K):
        out = out + gate_ref[:, k:k + 1] * buf[slot, k]
    o_ref[...] = out


def combine(x, y, pos, gate, rows=128):
    m, d = x.shape
    rows = min(rows, m)
    return pl.pallas_call(
        _combine_kernel,
        grid_spec=pltpu.PrefetchScalarGridSpec(
            num_scalar_prefetch=1,
            grid=(m // rows,),
            in_specs=[pl.BlockSpec((rows, d), lambda i, pos: (i, 0)),
                      pl.BlockSpec((rows, TOP_K), lambda i, pos: (i, 0)),
                      pl.BlockSpec(memory_space=pl.ANY)],
            out_specs=pl.BlockSpec((rows, d), lambda i, pos: (i, 0)),
            scratch_shapes=[pltpu.VMEM((2, TOP_K, rows, d), F32), pltpu.SemaphoreType.DMA((2,))],
        ),
        out_shape=jax.ShapeDtypeStruct((m, d), F32),
        compiler_params=_params("arbitrary"),
        name="combine",
    )(pos, x, gate.T, y)


def _routing_tables(idx, n_exp, tm):
    m = idx.shape[1]
    n_assign = TOP_K * m
    n_tiles = n_assign // tm + n_exp
    p = n_tiles * tm
    e_flat = idx.reshape(-1)
    onehot = (e_flat[:, None] == jnp.arange(n_exp)[None, :]).astype(jnp.int32)
    before = jnp.cumsum(onehot, axis=0) - onehot
    rank = jnp.sum(before * onehot, axis=1)
    counts = jnp.sum(onehot, axis=0)
    padded = ((counts + tm - 1) // tm) * tm
    ends = jnp.cumsum(padded)
    offsets = ends - padded
    pos = offsets[e_flat] + rank
    token = jnp.tile(jnp.arange(m, dtype=jnp.int32), TOP_K)
    src_rows = jnp.zeros((p,), jnp.int32).at[pos].set(token)
    tile_start = jnp.arange(n_tiles, dtype=jnp.int32) * tm
    tile_expert = jnp.sum((tile_start[:, None] >= ends[None, :]).astype(jnp.int32), axis=1)
    tile_valid = (tile_start < ends[-1]).astype(jnp.int32)
    tile_expert = jnp.minimum(tile_expert, n_exp - 1).astype(jnp.int32)
    return src_rows, pos.astype(jnp.int32), tile_expert, tile_valid


def _even_layer(x, seq, norm1, w_in, pool_w, pool_scale, sg_norm, sg_w, sg_b, w_out, norm2, w1, w3, w2,
                tm=512, tm_wide=1024, tn=512, tn_up=256):
    m, d = x.shape
    pool_width = pool_scale.shape[0]
    tm_wide = min(tm_wide, m)
    te, tv = _dense_tiles(m, tm)
    te_wide, tv_wide = _dense_tiles(m, tm_wide)
    h = rmsnorm(x, norm1)
    z = gmm(h, [w_in[None]], te_wide, tv_wide, tm=tm_wide, tn=tn, out_dtype=F32,
            epilogue=functools.partial(_ep_gelu_cols, pool_width // tn), name="w_in")
    y = mixer(z, pool_w, pool_scale, sg_norm, sg_w, sg_b, seq)
    x = gmm(y, [w_out[None]], te_wide, tv_wide, tm=tm_wide, tn=tn, out_dtype=F32, epilogue=_ep_residual,
            extras=[x], extra_specs=[_tile_spec(tm_wide, tn)], name="e_w_out")
    h = rmsnorm(x, norm2)
    hf = gmm(h, [w1[None], w3[None]], te_wide, tv_wide, tm=tm_wide, tn=tn_up, out_dtype=BF16, epilogue=_ep_swiglu,
             name="ffn_up")
    d_ff = w2.shape[0]
    w2h = w2.reshape(2, d_ff // 2, d)
    for half in range(2):
        x = gmm(hf, [w2h], jnp.full_like(te, half), tv, tm=tm, tn=tn, out_dtype=F32, epilogue=_ep_residual,
                extras=[x], extra_specs=[_tile_spec(tm, tn)], x_col_block=half, name="ffn_down")
    return x


def _odd_layer(x, batch, seq, norm1, w_qkv, q_norm, k_norm, w_out, norm2, router, w1, w3, w2,
               tm=512, tm_wide=1024, tn=512):
    m, d = x.shape
    heads = d // SB_HEAD_DIM
    tm_wide = min(tm_wide, m)
    te_wide, tv_wide = _dense_tiles(m, tm_wide)
    h = rmsnorm(x, norm1)
    gain_spec = pl.BlockSpec((1, SB_HEAD_DIM), lambda j, i, te, tv: (0, 0))
    q_gain = q_norm * (SB_HEAD_DIM ** -0.5 * LOG2E)
    qkv = gmm(h, [w_qkv[None]], te_wide, tv_wide, tm=tm_wide, tn=tn, out_dtype=BF16,
              epilogue=functools.partial(_ep_qkv, d // tn, d // tn),
              extras=[q_gain.reshape(1, -1), k_norm.reshape(1, -1)], extra_specs=[gain_spec, gain_spec],
              name="w_qkv")
    o = sb_attention(qkv, batch, seq, heads)
    x = gmm(o, [w_out[None]], te_wide, tv_wide, tm=tm_wide, tn=tn, out_dtype=F32, epilogue=_ep_residual,
            extras=[x], extra_specs=[_tile_spec(tm_wide, tn)], name="o_w_out")

    n_exp = router.shape[1]
    idx, gate, h_packed = router_top2(x, norm2, router)
    src_rows, pos, tile_expert, tile_valid = _routing_tables(idx, n_exp, tm)
    chunks = d // 2 // LANES
    xs = gather_rows(h_packed, src_rows, chunks)
    hs = gmm(xs, [w1, w3], tile_expert, tile_valid, tm=tm, tn=tn, out_dtype=BF16, epilogue=_ep_swiglu,
             name="moe_up")
    ys = gmm(hs, [w2], tile_expert, tile_valid, tm=tm, tn=tn, out_dtype=F32, epilogue=_ep_plain,
             name="moe_down")
    return combine(x, ys, pos, gate)


def kernel(x, e_norm1, e_w_in, e_pool_w, e_pool_scale, e_sg_norm, e_sg_w, e_sg_b, e_w_out, e_norm2, e_ffn_w1, e_ffn_w3, e_ffn_w2, o_norm1, o_w_qkv, o_q_norm, o_k_norm, o_w_out, o_norm2, o_router, o_moe_w1, o_moe_w3, o_moe_w2):
    batch, seq, d = x.shape
    depth = e_norm1.shape[0] + o_norm1.shape[0]
    xf = x.reshape(batch * seq, d)
    for layer in range(depth):
        i = layer // 2
        if layer % 2 == 0:
            xf = _even_layer(xf, seq, e_norm1[i], e_w_in[i], e_pool_w[i], e_pool_scale[i], e_sg_norm[i],
                             e_sg_w[i], e_sg_b[i], e_w_out[i], e_norm2[i], e_ffn_w1[i], e_ffn_w3[i], e_ffn_w2[i])
        else:
            xf = _odd_layer(xf, batch, seq, o_norm1[i], o_w_qkv[i], o_q_norm[i], o_k_norm[i], o_w_out[i],
                            o_norm2[i], o_router[i], o_moe_w1[i], o_moe_w3[i], o_moe_w2[i])
    return xf.reshape(batch, seq, d)
```

```python
import functools

import jax
import jax.numpy as jnp
from jax import lax
from jax.experimental import pallas as pl
from jax.experimental.pallas import tpu as pltpu

EPS = 1e-6
CHUNK = 64
SG_LEN = 128
SG_HEADS = 8
POOL_WINDOWS = (2, 4, 8, 16)
POOL_HALO = 16
SB_HEAD_DIM = 128
SB_EXIT_LOG2 = -150.0
LOG2E = 1.4426950408889634
TOP_K = 2

V7X_VMEM_BYTES = 64 * 1024 * 1024
VMEM_LIMIT_BYTES = V7X_VMEM_BYTES - 8 * 1024 * 1024
LANES = 128

F32 = jnp.float32
BF16 = jnp.bfloat16


def _params(*semantics):
    return pltpu.CompilerParams(dimension_semantics=semantics, vmem_limit_bytes=VMEM_LIMIT_BYTES)


def _rmsnorm_kernel(x_ref, g_ref, o_ref):
    x = x_ref[...]
    ms = jnp.mean(x * x, axis=-1, keepdims=True)
    o_ref[...] = (x * lax.rsqrt(ms + EPS) * g_ref[...]).astype(o_ref.dtype)


def rmsnorm(x, gain, tm=256):
    m, d = x.shape
    tm = min(tm, m)
    return pl.pallas_call(
        _rmsnorm_kernel,
        grid=(m // tm,),
        in_specs=[pl.BlockSpec((tm, d), lambda i: (i, 0)),
                  pl.BlockSpec((1, d), lambda i: (0, 0))],
        out_specs=pl.BlockSpec((tm, d), lambda i: (i, 0)),
        out_shape=jax.ShapeDtypeStruct((m, d), BF16),
        compiler_params=_params("arbitrary"),
        name="rmsnorm",
    )(x, gain.reshape(1, d))


def _gmm_kernel(te_ref, tv_ref, x_ref, *refs, n_w, n_extra, epilogue):
    del te_ref
    w_refs = refs[:n_w]
    extra_refs = refs[n_w:n_w + n_extra]
    o_ref = refs[n_w + n_extra]
    j = pl.program_id(0)
    i = pl.program_id(1)

    @pl.when(tv_ref[i] != 0)
    def _():
        x = x_ref[...]
        accs = [jnp.dot(x, w[0].astype(BF16), preferred_element_type=F32) for w in w_refs]
        epilogue(accs, extra_refs, o_ref, j)

    @pl.when(tv_ref[i] == 0)
    def _():
        o_ref[...] = jnp.zeros(o_ref.shape, o_ref.dtype)


def gmm(x, weights, tile_expert, tile_valid, *, tm, tn, epilogue, out_dtype,
        extras=(), extra_specs=(), x_col_block=0, name="gmm"):
    p = x.shape[0]
    _, k, n = weights[0].shape
    grid = (pl.cdiv(n, tn), p // tm)
    x_spec = pl.BlockSpec((tm, k), lambda j, i, te, tv: (i, x_col_block))
    w_spec = pl.BlockSpec((1, k, tn), lambda j, i, te, tv: (te[i], 0, j))
    kern = functools.partial(_gmm_kernel, n_w=len(weights), n_extra=len(extras), epilogue=epilogue)
    return pl.pallas_call(
        kern,
        grid_spec=pltpu.PrefetchScalarGridSpec(
            num_scalar_prefetch=2,
            grid=grid,
            in_specs=[x_spec] + [w_spec] * len(weights) + list(extra_specs),
            out_specs=pl.BlockSpec((tm, tn), lambda j, i, te, tv: (i, j)),
        ),
        out_shape=jax.ShapeDtypeStruct((p, n), out_dtype),
        compiler_params=_params("arbitrary", "arbitrary"),
        name=name,
    )(tile_expert, tile_valid, x, *weights, *extras)


def _gmm_moe_kernel(te_ref, tv_ref, grp_ref, nexte_ref, ngroups_ref, xrow_ref, x_ref, *refs,
                    n_w, epilogue, tn, last_width):
    del xrow_ref
    w_hbm = refs[:n_w]
    o_ref = refs[n_w]
    wbuf = refs[n_w + 1:2 * n_w + 1]
    sems = refs[2 * n_w + 1]
    j = pl.program_id(0)
    i = pl.program_id(1)
    nj = pl.num_programs(0)
    valid = tv_ref[i] != 0
    starts_group = valid & ((i == 0) | (te_ref[i] != te_ref[jnp.maximum(i - 1, 0)]))
    slot = (j * ngroups_ref[0] + grp_ref[i]) % 2

    def block_copies(expert, jj, slot, width):
        col0 = pl.multiple_of(jj * tn, tn)
        return [pltpu.make_async_copy(w_hbm[k].at[expert, :, pl.ds(col0, width)],
                                      wbuf[k].at[slot, :, pl.ds(0, width)], sems.at[slot, k])
                for k in range(n_w)]

    def for_block(expert, jj, slot, action):
        if last_width == tn:
            for c in block_copies(expert, jj, slot, tn):
                action(c)
        else:
            @pl.when(jj < nj - 1)
            def _():
                for c in block_copies(expert, jj, slot, tn):
                    action(c)

            @pl.when(jj == nj - 1)
            def _():
                for c in block_copies(expert, jj, slot, last_width):
                    action(c)

    def start(c):
        c.start()

    def wait(c):
        c.wait()

    @pl.when(starts_group)
    def _():
        @pl.when((j == 0) & (i == 0))
        def _():
            for_block(te_ref[i], j, slot, start)

        for_block(te_ref[i], j, slot, wait)
        next_expert = nexte_ref[i]

        @pl.when(next_expert >= 0)
        def _():
            for_block(next_expert, j, 1 - slot, start)

        @pl.when((next_expert < 0) & (j + 1 < nj))
        def _():
            for_block(te_ref[0], j + 1, 1 - slot, start)

    @pl.when(valid)
    def _():
        x = x_ref[...]
        accs = [jnp.dot(x, wb[slot].astype(BF16), preferred_element_type=F32) for wb in wbuf]
        epilogue(accs, (), o_ref, j)

    @pl.when(jnp.logical_not(valid))
    def _():
        o_ref[...] = jnp.zeros(o_ref.shape, o_ref.dtype)


def gmm_moe(x, weights, tables, *, tm, tn, epilogue, out_dtype, name):
    p, k = x.shape
    _, _, n = weights[0].shape
    nj = pl.cdiv(n, tn)
    n_w = len(weights)
    kern = functools.partial(_gmm_moe_kernel, n_w=n_w, epilogue=epilogue, tn=tn,
                             last_width=n - (nj - 1) * tn)
    return pl.pallas_call(
        kern,
        grid_spec=pltpu.PrefetchScalarGridSpec(
            num_scalar_prefetch=len(tables),
            grid=(nj, p // tm),
            in_specs=[pl.BlockSpec((tm, k), lambda j, i, te, tv, grp, ne, ng, xrow: (xrow[i], 0))]
                     + [pl.BlockSpec(memory_space=pl.ANY)] * n_w,
            out_specs=pl.BlockSpec((tm, tn), lambda j, i, *_: (i, j)),
            scratch_shapes=[pltpu.VMEM((2, k, tn), F32)] * n_w + [pltpu.SemaphoreType.DMA((2, n_w))],
        ),
        out_shape=jax.ShapeDtypeStruct((p, n), out_dtype),
        compiler_params=_params("arbitrary", "arbitrary"),
        name=name,
    )(*tables, x, *weights)


def _dense_tiles(m, tm):
    n = m // tm
    return jnp.zeros((n,), jnp.int32), jnp.ones((n,), jnp.int32)


def _tile_spec(tm, tn):
    return pl.BlockSpec((tm, tn), lambda j, i, te, tv: (i, j))


def _ep_gelu_cols(first_gelu_block, accs, extra_refs, o_ref, j):
    del extra_refs
    acc = accs[0]

    @pl.when(j < first_gelu_block)
    def _():
        o_ref[...] = acc.astype(o_ref.dtype)

    @pl.when(j >= first_gelu_block)
    def _():
        o_ref[...] = jax.nn.gelu(acc).astype(o_ref.dtype)


def _ep_residual(accs, extra_refs, o_ref, j):
    del j
    o_ref[...] = (extra_refs[0][...] + accs[0]).astype(o_ref.dtype)


def _ep_swiglu(accs, extra_refs, o_ref, j):
    del extra_refs, j
    a, g = accs
    o_ref[...] = (jax.nn.silu(a) * g).astype(o_ref.dtype)


def _ep_plain(accs, extra_refs, o_ref, j):
    del extra_refs, j
    o_ref[...] = accs[0].astype(o_ref.dtype)


def _ep_qkv(q_blocks, k_blocks, accs, extra_refs, o_ref, j):
    acc = accs[0]
    qg_ref, kg_ref = extra_refs
    tn = acc.shape[1]

    @pl.when(j < q_blocks + k_blocks)
    def _():
        gain = jnp.where(j < q_blocks, qg_ref[...], kg_ref[...])
        for c in range(tn // SB_HEAD_DIM):
            sl = slice(c * SB_HEAD_DIM, (c + 1) * SB_HEAD_DIM)
            a = acc[:, sl]
            ms = jnp.mean(a * a, axis=-1, keepdims=True)
            o_ref[:, sl] = (a * lax.rsqrt(ms + EPS) * gain).astype(o_ref.dtype)

    @pl.when(j >= q_blocks + k_blocks)
    def _():
        o_ref[...] = acc.astype(o_ref.dtype)


def _mixer_kernel(za_ref, halo_ref, u_ref, v_ref, pw_ref, ps_ref, sgn_ref, sgw_ref, sgb_ref,
                  o_ref, zbuf, *, tiles_per_seq):
    i = pl.program_id(0)
    ts, pool_width = za_ref.shape
    sg_width = v_ref.shape[1]
    group_width = pool_width // len(POOL_WINDOWS)
    head_dim = sg_width // SG_HEADS
    tile_in_seq = i % tiles_per_seq

    za = za_ref[...]
    zbuf[POOL_HALO:POOL_HALO + ts, :] = za
    zbuf[0:POOL_HALO, :] = jnp.where(tile_in_seq == 0, 0.0, halo_ref[...])
    n_avail = (lax.broadcasted_iota(jnp.int32, (ts, 1), 0) + tile_in_seq * ts + 1).astype(F32)
    for g, w in enumerate(POOL_WINDOWS):
        cols = slice(g * group_width, (g + 1) * group_width)
        win = za[:, cols]
        for back in range(1, w):
            win = win + zbuf[POOL_HALO - back:POOL_HALO - back + ts, cols]
        mean = win / jnp.minimum(n_avail, float(w))
        pooled = (mean - za[:, cols]).astype(BF16)
        ya = jnp.dot(pooled, pw_ref[g], preferred_element_type=F32) * ps_ref[:, cols]
        o_ref[:, cols] = ya.astype(o_ref.dtype)

    v = v_ref[...]
    ms = jnp.mean(v * v, axis=-1, keepdims=True)
    vn = (v * lax.rsqrt(ms + EPS) * sgn_ref[...]).astype(BF16)
    pos_i = lax.broadcasted_iota(jnp.int32, (SG_LEN, SG_LEN), 0)
    pos_j = lax.broadcasted_iota(jnp.int32, (SG_LEN, SG_LEN), 1)
    allowed = (pos_j // CHUNK) <= (pos_i // CHUNK)
    for h in range(SG_HEADS):
        w_h = jnp.where(allowed, sgw_ref[h], 0.0).astype(BF16)
        hc = slice(h * head_dim, (h + 1) * head_dim)
        for c in range(ts // SG_LEN):
            rows = slice(c * SG_LEN, (c + 1) * SG_LEN)
            mixed = jnp.dot(w_h, vn[rows, hc], preferred_element_type=F32) + sgb_ref[:, hc]
            yb = u_ref[rows, hc] * mixed
            o_ref[rows, pool_width + h * head_dim:pool_width + (h + 1) * head_dim] = yb.astype(o_ref.dtype)


def mixer(z, pool_w, pool_scale, sg_norm, sg_w, sg_b, seq, ts=256):
    m = z.shape[0]
    pool_width = pool_scale.shape[0]
    sg_width = sg_norm.shape[0]
    assert pool_width == sg_width
    ts = min(ts, seq)
    tiles_per_seq = seq // ts
    halo_blocks_per_tile = ts // POOL_HALO
    head_dim = sg_width // SG_HEADS
    bias = jnp.repeat(sg_b.T, head_dim, axis=1)
    kern = functools.partial(_mixer_kernel, tiles_per_seq=tiles_per_seq)
    const2 = lambda i: (0, 0)
    const3 = lambda i: (0, 0, 0)
    return pl.pallas_call(
        kern,
        grid=(m // ts,),
        in_specs=[
            pl.BlockSpec((ts, pool_width), lambda i: (i, 0)),
            pl.BlockSpec((POOL_HALO, pool_width), lambda i: (jnp.maximum(i * halo_blocks_per_tile - 1, 0), 0)),
            pl.BlockSpec((ts, sg_width), lambda i: (i, 1)),
            pl.BlockSpec((ts, sg_width), lambda i: (i, 2)),
            pl.BlockSpec(pool_w.shape, const3),
            pl.BlockSpec((1, pool_width), const2),
            pl.BlockSpec((1, sg_width), const2),
            pl.BlockSpec(sg_w.shape, const3),
            pl.BlockSpec(bias.shape, const2),
        ],
        out_specs=pl.BlockSpec((ts, pool_width + sg_width), lambda i: (i, 0)),
        out_shape=jax.ShapeDtypeStruct((m, pool_width + sg_width), BF16),
        scratch_shapes=[pltpu.VMEM((ts + POOL_HALO, pool_width), F32)],
        compiler_params=_params("arbitrary"),
        name="mixer",
    )(z, z, z, z, pool_w.astype(BF16), pool_scale.reshape(1, -1), sg_norm.reshape(1, -1), sg_w, bias)


def _attn_kernel(q_ref, k_ref, v_ref, o_ref, *, group):
    qi = pl.program_id(2)
    tq = q_ref.shape[0]
    t_pos = lax.broadcasted_iota(jnp.int32, (tq, tq), 0)
    s_pos = lax.broadcasted_iota(jnp.int32, (tq, tq), 1)
    strict = s_pos < t_pos
    neg_tri = jnp.where(strict, -1.0, 0.0).astype(BF16)

    def head(g):
        return slice(g * SB_HEAD_DIM, (g + 1) * SB_HEAD_DIM)

    def sweep(k0, accs, carries, mask):
        zs, softpluses = [], []
        for g in range(group):
            z = lax.dot_general(q_ref[:, head(g)], k_ref[pl.ds(k0, tq), head(g)],
                                (((1,), (1,)), ((), ())), preferred_element_type=F32)
            neg_abs = lax.bitcast_convert_type(
                lax.bitcast_convert_type(z, jnp.uint32) | jnp.uint32(0x80000000), F32)
            softplus = jnp.maximum(z, 0.0) + jnp.log(1.0 + jnp.exp2(neg_abs)) * LOG2E
            if mask is not None:
                softplus = jnp.where(mask, softplus, 0.0)
            zs.append(z)
            softpluses.append(softplus)
        within = jnp.dot(jnp.concatenate([sp.astype(BF16) for sp in softpluses], axis=0), neg_tri,
                         preferred_element_type=F32)
        new_accs, new_carries = [], []
        for g in range(group):
            after = within[g * tq:(g + 1) * tq] + carries[g]
            attn = jnp.exp2(zs[g] - softpluses[g] + after)
            if mask is not None:
                attn = jnp.where(mask, attn, 0.0)
            out = jnp.dot(attn.astype(BF16), v_ref[pl.ds(k0, tq), head(g)], preferred_element_type=F32)
            new_accs.append(out if accs is None else accs[g] + out)
            new_carries.append(carries[g] - jnp.sum(softpluses[g], axis=1, keepdims=True))
        return tuple(new_accs), tuple(new_carries)

    def alive(carries):
        top = functools.reduce(jnp.maximum, carries)
        return (jnp.max(top) > SB_EXIT_LOG2).astype(jnp.int32)

    zero = jnp.zeros((tq, 1), F32)
    accs, carries = sweep(pl.multiple_of(qi * tq, tq), None, (zero,) * group, strict)

    def cond(state):
        n, live = state[0], state[1]
        return jnp.logical_and(n < qi, live > 0)

    def body(state):
        n, _, accs, carries = state
        accs, carries = sweep(pl.multiple_of((qi - 1 - n) * tq, tq), accs, carries, None)
        return n + 1, alive(carries), accs, carries

    _, _, accs, _ = lax.while_loop(cond, body, (jnp.int32(0), alive(carries), accs, carries))
    for g in range(group):
        o_ref[:, head(g)] = accs[g].astype(o_ref.dtype)


def sb_attention(qkv, batch, seq, heads, tq=256, group=8):
    m = qkv.shape[0]
    tq = min(tq, seq)
    group = min(group, heads)
    nq = seq // tq
    n_groups = heads // group
    width = group * SB_HEAD_DIM
    kern = functools.partial(_attn_kernel, group=group)
    return pl.pallas_call(
        kern,
        grid=(batch, n_groups, nq),
        in_specs=[
            pl.BlockSpec((tq, width), lambda b, h, i: (b * nq + i, h)),
            pl.BlockSpec((seq, width), lambda b, h, i: (b, n_groups + h)),
            pl.BlockSpec((seq, width), lambda b, h, i: (b, 2 * n_groups + h)),
        ],
        out_specs=pl.BlockSpec((tq, width), lambda b, h, i: (b * nq + i, h)),
        out_shape=jax.ShapeDtypeStruct((m, heads * SB_HEAD_DIM), BF16),
        compiler_params=_params("arbitrary", "arbitrary", "arbitrary"),
        name="sb_attention",
    )(qkv, qkv, qkv)


def _router_kernel(x_ref, g_ref, rt_ref, idx_ref, gate_ref, hp_ref):
    x = x_ref[...]
    tm, d = x.shape
    ms = jnp.mean(x * x, axis=-1, keepdims=True)
    h = x * lax.rsqrt(ms + EPS) * g_ref[...]
    logits = lax.dot_general(rt_ref[...], h, (((1,), (1,)), ((), ())),
                             precision=lax.Precision.HIGHEST, preferred_element_type=F32)
    n_exp = logits.shape[0]
    eid = lax.broadcasted_iota(jnp.int32, logits.shape, 0)
    m1 = jnp.max(logits, axis=0, keepdims=True)
    i1 = jnp.min(jnp.where(logits == m1, eid, n_exp), axis=0, keepdims=True)
    rest = jnp.where(eid == i1, -jnp.inf, logits)
    m2 = jnp.max(rest, axis=0, keepdims=True)
    i2 = jnp.min(jnp.where(rest == m2, eid, n_exp), axis=0, keepdims=True)
    e2 = jnp.exp(m2 - m1)
    g1 = 1.0 / (1.0 + e2)
    idx_ref[...] = jnp.concatenate([i1, i2], axis=0)
    gate_ref[...] = jnp.concatenate([g1, e2 * g1], axis=0)

    half = d // 2
    chunks = half // LANES
    lo = lax.bitcast_convert_type(h[:, :half].astype(BF16).astype(F32), jnp.uint32)
    hi = lax.bitcast_convert_type(h[:, half:].astype(BF16).astype(F32), jnp.uint32)
    words = hi | (lo >> 16)
    for c in range(chunks):
        hp_ref[pl.ds(c, tm, stride=chunks), :] = words[:, c * LANES:(c + 1) * LANES]


def router_top2(x, gain, router, tm=256):
    m, d = x.shape
    n_exp = router.shape[1]
    tm = min(tm, m)
    chunks = d // 2 // LANES
    return pl.pallas_call(
        _router_kernel,
        grid=(m // tm,),
        in_specs=[pl.BlockSpec((tm, d), lambda i: (i, 0)),
                  pl.BlockSpec((1, d), lambda i: (0, 0)),
                  pl.BlockSpec((n_exp, d), lambda i: (0, 0))],
        out_specs=[pl.BlockSpec((TOP_K, tm), lambda i: (0, i)),
                   pl.BlockSpec((TOP_K, tm), lambda i: (0, i)),
                   pl.BlockSpec((tm * chunks, LANES), lambda i: (i, 0))],
        out_shape=[jax.ShapeDtypeStruct((TOP_K, m), jnp.int32),
                   jax.ShapeDtypeStruct((TOP_K, m), F32),
                   jax.ShapeDtypeStruct((m * chunks, LANES), jnp.uint32)],
        compiler_params=_params("arbitrary"),
        name="router_top2",
    )(x, gain.reshape(1, d), router.T)


ISSUE_UNROLL = 8


def _row_start(row, height):
    if height == 1 or isinstance(row, int):
        return row * height
    return pl.multiple_of(row * height, height)


def _row_copy(src_hbm, src_row, dst_vmem, dst_row, sem, height):
    return pltpu.make_async_copy(src_hbm.at[pl.ds(_row_start(src_row, height), height), :],
                                 dst_vmem.at[pl.ds(_row_start(dst_row, height), height), :], sem)


def _issue_rows(src_hbm, index_of_row, dst_vmem, sem, height=1):
    def group(b, c):
        for u in range(ISSUE_UNROLL):
            r = b * ISSUE_UNROLL + u
            _row_copy(src_hbm, index_of_row(r), dst_vmem, r, sem, height).start(priority=u % 2)
        return c
    lax.fori_loop(0, dst_vmem.shape[0] // height // ISSUE_UNROLL, group, 0)


def _wait_rows(src_hbm, dst_vmem, sem, height=1):
    def one(r, c):
        _row_copy(src_hbm, 0, dst_vmem, r, sem, height).wait()
        return c
    lax.fori_loop(0, dst_vmem.shape[0] // height, one, 0, unroll=ISSUE_UNROLL)


def _unpack_rows(x_ref, tm, chunks):
    words = jnp.concatenate([x_ref[pl.ds(c, tm, stride=chunks), :] for c in range(chunks)], axis=1)
    lo = lax.bitcast_convert_type(words << 16, F32).astype(BF16)
    hi = lax.bitcast_convert_type(words & jnp.uint32(0xFFFF0000), F32).astype(BF16)
    return jnp.concatenate([lo, hi], axis=1)


def _gather_kernel(src_ref, x_hbm, o_ref, buf, sems, *, height):
    i = pl.program_id(0)
    rows = o_ref.shape[0]
    slot = i % 2

    def issue(tile, slot):
        _issue_rows(x_hbm, lambda r: src_ref[tile * rows + r], buf.at[slot], sems.at[slot], height)

    @pl.when(i == 0)
    def _():
        issue(0, 0)

    @pl.when(i + 1 < pl.num_programs(0))
    def _():
        issue(i + 1, 1 - slot)

    _wait_rows(x_hbm, buf.at[slot], sems.at[slot], height)
    o_ref[...] = _unpack_rows(buf.at[slot], rows, height)


def gather_rows(x, src_rows, height, rows=256):
    p = src_rows.shape[0]
    kern = functools.partial(_gather_kernel, height=height)
    return pl.pallas_call(
        kern,
        grid_spec=pltpu.PrefetchScalarGridSpec(
            num_scalar_prefetch=1,
            grid=(p // rows,),
            in_specs=[pl.BlockSpec(memory_space=pl.ANY)],
            out_specs=pl.BlockSpec((rows, 2 * LANES * height), lambda i, src: (i, 0)),
            scratch_shapes=[pltpu.VMEM((2, rows * height, LANES), x.dtype), pltpu.SemaphoreType.DMA((2,))],
        ),
        out_shape=jax.ShapeDtypeStruct((p, 2 * LANES * height), BF16),
        compiler_params=_params("arbitrary"),
        name="gather_rows",
    )(src_rows, x)


def _combine_kernel(pos_ref, x_ref, gate_ref, y_hbm, o_ref, buf, sems):
    i = pl.program_id(0)
    rows = x_ref.shape[0]
    n_tokens = pos_ref.shape[0] // TOP_K
    slot = i % 2

    def issue(tile, slot):
        for k in range(TOP_K):
            _issue_rows(y_hbm, lambda r: pos_ref[k * n_tokens + tile * rows + r],
                        buf.at[slot, k], sems.at[slot])

    @pl.when(i == 0)
    def _():
        issue(0, 0)

    @pl.when(i + 1 < pl.num_programs(0))
    def _():
        issue(i + 1, 1 - slot)

    for k in range(TOP_K):
        _wait_rows(y_hbm, buf.at[slot, k], sems.at[slot])
    out = x_ref[...]
    for k in range(TOP_K):
        out = out + gate_ref[:, k:k + 1] * buf[slot, k]
    o_ref[...] = out


def combine(x, y, pos, gate, rows=128):
    m, d = x.shape
    rows = min(rows, m)
    return pl.pallas_call(
        _combine_kernel,
        grid_spec=pltpu.PrefetchScalarGridSpec(
            num_scalar_prefetch=1,
            grid=(m // rows,),
            in_specs=[pl.BlockSpec((rows, d), lambda i, pos: (i, 0)),
                      pl.BlockSpec((rows, TOP_K), lambda i, pos: (i, 0)),
                      pl.BlockSpec(memory_space=pl.ANY)],
            out_specs=pl.BlockSpec((rows, d), lambda i, pos: (i, 0)),
            scratch_shapes=[pltpu.VMEM((2, TOP_K, rows, d), F32), pltpu.SemaphoreType.DMA((2,))],
        ),
        out_shape=jax.ShapeDtypeStruct((m, d), F32),
        compiler_params=_params("arbitrary"),
        name="combine",
    )(pos, x, gate.T, y)


def _routing_tables(idx, n_exp, tm):
    m = idx.shape[1]
    n_assign = TOP_K * m
    n_tiles = n_assign // tm + n_exp
    p = n_tiles * tm
    e_flat = idx.reshape(-1)
    onehot = (e_flat[:, None] == jnp.arange(n_exp)[None, :]).astype(jnp.int32)
    before = jnp.cumsum(onehot, axis=0) - onehot
    rank = jnp.sum(before * onehot, axis=1)
    counts = jnp.sum(onehot, axis=0)
    padded = ((counts + tm - 1) // tm) * tm
    ends = jnp.cumsum(padded)
    offsets = ends - padded
    pos = offsets[e_flat] + rank
    token = jnp.tile(jnp.arange(m, dtype=jnp.int32), TOP_K)
    src_rows = jnp.zeros((p,), jnp.int32).at[pos].set(token)
    tile_start = jnp.arange(n_tiles, dtype=jnp.int32) * tm
    tile_expert = jnp.sum((tile_start[:, None] >= ends[None, :]).astype(jnp.int32), axis=1)
    tile_valid = (tile_start < ends[-1]).astype(jnp.int32)
    tile_expert = jnp.minimum(tile_expert, n_exp - 1).astype(jnp.int32)
    experts = jnp.arange(n_exp, dtype=jnp.int32)
    present = counts > 0
    later_present = (experts[None, :] > experts[:, None]) & present[None, :]
    next_present = jnp.min(jnp.where(later_present, experts[None, :], n_exp), axis=1)
    next_present = jnp.where(next_present == n_exp, -1, next_present).astype(jnp.int32)
    group_of_expert = (jnp.cumsum(present.astype(jnp.int32)) - 1).astype(jnp.int32)
    n_groups = jnp.sum(present.astype(jnp.int32)).reshape(1)
    tile_index = jnp.arange(n_tiles, dtype=jnp.int32)
    x_row_block = jnp.minimum(tile_index, jnp.sum(tile_valid) - 1).astype(jnp.int32)
    tables = (tile_expert, tile_valid, group_of_expert[tile_expert], next_present[tile_expert], n_groups,
              x_row_block)
    return src_rows, pos.astype(jnp.int32), tables


def _even_layer(x, seq, norm1, w_in, pool_w, pool_scale, sg_norm, sg_w, sg_b, w_out, norm2, w1, w3, w2,
                tm=512, tm_wide=1024, tn=512, tn_up=256):
    m, d = x.shape
    pool_width = pool_scale.shape[0]
    tm_wide = min(tm_wide, m)
    te, tv = _dense_tiles(m, tm)
    te_wide, tv_wide = _dense_tiles(m, tm_wide)
    h = rmsnorm(x, norm1)
    z = gmm(h, [w_in[None]], te_wide, tv_wide, tm=tm_wide, tn=tn, out_dtype=F32,
            epilogue=functools.partial(_ep_gelu_cols, pool_width // tn), name="w_in")
    y = mixer(z, pool_w, pool_scale, sg_norm, sg_w, sg_b, seq)
    x = gmm(y, [w_out[None]], te_wide, tv_wide, tm=tm_wide, tn=tn, out_dtype=F32, epilogue=_ep_residual,
            extras=[x], extra_specs=[_tile_spec(tm_wide, tn)], name="e_w_out")
    h = rmsnorm(x, norm2)
    hf = gmm(h, [w1[None], w3[None]], te_wide, tv_wide, tm=tm_wide, tn=tn_up, out_dtype=BF16, epilogue=_ep_swiglu,
             name="ffn_up")
    d_ff = w2.shape[0]
    w2h = w2.reshape(2, d_ff // 2, d)
    for half in range(2):
        x = gmm(hf, [w2h], jnp.full_like(te, half), tv, tm=tm, tn=tn, out_dtype=F32, epilogue=_ep_residual,
                extras=[x], extra_specs=[_tile_spec(tm, tn)], x_col_block=half, name="ffn_down")
    return x


def _odd_layer(x, batch, seq, norm1, w_qkv, q_norm, k_norm, w_out, norm2, router, w1, w3, w2,
               tm=512, tm_wide=1024, tn=512):
    m, d = x.shape
    heads = d // SB_HEAD_DIM
    tm_wide = min(tm_wide, m)
    te_wide, tv_wide = _dense_tiles(m, tm_wide)
    h = rmsnorm(x, norm1)
    gain_spec = pl.BlockSpec((1, SB_HEAD_DIM), lambda j, i, te, tv: (0, 0))
    q_gain = q_norm * (SB_HEAD_DIM ** -0.5 * LOG2E)
    qkv = gmm(h, [w_qkv[None]], te_wide, tv_wide, tm=tm_wide, tn=tn, out_dtype=BF16,
              epilogue=functools.partial(_ep_qkv, d // tn, d // tn),
              extras=[q_gain.reshape(1, -1), k_norm.reshape(1, -1)], extra_specs=[gain_spec, gain_spec],
              name="w_qkv")
    o = sb_attention(qkv, batch, seq, heads)
    x = gmm(o, [w_out[None]], te_wide, tv_wide, tm=tm_wide, tn=tn, out_dtype=F32, epilogue=_ep_residual,
            extras=[x], extra_specs=[_tile_spec(tm_wide, tn)], name="o_w_out")

    n_exp = router.shape[1]
    idx, gate, h_packed = router_top2(x, norm2, router)
    src_rows, pos, tables = _routing_tables(idx, n_exp, tm)
    chunks = d // 2 // LANES
    xs = gather_rows(h_packed, src_rows, chunks)
    hs = gmm_moe(xs, [w1, w3], tables, tm=tm, tn=tn, out_dtype=BF16, epilogue=_ep_swiglu, name="moe_up")
    ys = gmm_moe(hs, [w2], tables, tm=tm, tn=tn, out_dtype=F32, epilogue=_ep_plain, name="moe_down")
    return combine(x, ys, pos, gate)


def kernel(x, e_norm1, e_w_in, e_pool_w, e_pool_scale, e_sg_norm, e_sg_w, e_sg_b, e_w_out, e_norm2, e_ffn_w1, e_ffn_w3, e_ffn_w2, o_norm1, o_w_qkv, o_q_norm, o_k_norm, o_w_out, o_norm2, o_router, o_moe_w1, o_moe_w3, o_moe_w2):
    batch, seq, d = x.shape
    depth = e_norm1.shape[0] + o_norm1.shape[0]
    xf = x.reshape(batch * seq, d)
    for layer in range(depth):
        i = layer // 2
        if layer % 2 == 0:
            xf = _even_layer(xf, seq, e_norm1[i], e_w_in[i], e_pool_w[i], e_pool_scale[i], e_sg_norm[i],
                             e_sg_w[i], e_sg_b[i], e_w_out[i], e_norm2[i], e_ffn_w1[i], e_ffn_w3[i], e_ffn_w2[i])
        else:
            xf = _odd_layer(xf, batch, seq, o_norm1[i], o_w_qkv[i], o_q_norm[i], o_k_norm[i], o_w_out[i],
                            o_norm2[i], o_router[i], o_moe_w1[i], o_moe_w3[i], o_moe_w2[i])
    return xf.reshape(batch, seq, d)
```

```python
import functools

import jax
import jax.numpy as jnp
from jax import lax
from jax.experimental import pallas as pl
from jax.experimental.pallas import tpu as pltpu

EPS = 1e-6
CHUNK = 64
SG_LEN = 128
SG_HEADS = 8
POOL_WINDOWS = (2, 4, 8, 16)
POOL_HALO = 16
SB_HEAD_DIM = 128
SB_EXIT_LOG2 = -150.0
LOG2E = 1.4426950408889634
TOP_K = 2

V7X_VMEM_BYTES = 64 * 1024 * 1024
VMEM_LIMIT_BYTES = V7X_VMEM_BYTES - 8 * 1024 * 1024
LANES = 128

F32 = jnp.float32
BF16 = jnp.bfloat16


def _params(*semantics):
    return pltpu.CompilerParams(dimension_semantics=semantics, vmem_limit_bytes=VMEM_LIMIT_BYTES)


def _rmsnorm_kernel(x_ref, g_ref, o_ref):
    x = x_ref[...]
    ms = jnp.mean(x * x, axis=-1, keepdims=True)
    o_ref[...] = (x * lax.rsqrt(ms + EPS) * g_ref[...]).astype(o_ref.dtype)


def rmsnorm(x, gain, tm=256):
    m, d = x.shape
    tm = min(tm, m)
    return pl.pallas_call(
        _rmsnorm_kernel,
        grid=(m // tm,),
        in_specs=[pl.BlockSpec((tm, d), lambda i: (i, 0)),
                  pl.BlockSpec((1, d), lambda i: (0, 0))],
        out_specs=pl.BlockSpec((tm, d), lambda i: (i, 0)),
        out_shape=jax.ShapeDtypeStruct((m, d), BF16),
        compiler_params=_params("arbitrary"),
        name="rmsnorm",
    )(x, gain.reshape(1, d))


def _gmm_kernel(te_ref, tv_ref, grp_ref, nexte_ref, ngroups_ref, xrow_ref, x_ref, *refs,
                n_w, n_extra, epilogue, tn, last_width):
    del xrow_ref
    w_hbm = refs[:n_w]
    extra_refs = refs[n_w:n_w + n_extra]
    o_ref = refs[n_w + n_extra]
    wbuf = refs[n_w + n_extra + 1:2 * n_w + n_extra + 1]
    sems = refs[2 * n_w + n_extra + 1]
    j = pl.program_id(0)
    i = pl.program_id(1)
    nj = pl.num_programs(0)
    valid = tv_ref[i] != 0
    starts_group = valid & ((i == 0) | (te_ref[i] != te_ref[jnp.maximum(i - 1, 0)]))
    slot = (j * ngroups_ref[0] + grp_ref[i]) % 2

    def block_copies(expert, jj, slot, width):
        col0 = pl.multiple_of(jj * tn, tn)
        return [pltpu.make_async_copy(w_hbm[k].at[expert, :, pl.ds(col0, width)],
                                      wbuf[k].at[slot, :, pl.ds(0, width)], sems.at[slot, k])
                for k in range(n_w)]

    def for_block(expert, jj, slot, action):
        if last_width == tn:
            for c in block_copies(expert, jj, slot, tn):
                action(c)
        else:
            @pl.when(jj < nj - 1)
            def _():
                for c in block_copies(expert, jj, slot, tn):
                    action(c)

            @pl.when(jj == nj - 1)
            def _():
                for c in block_copies(expert, jj, slot, last_width):
                    action(c)

    def start(c):
        c.start()

    def wait(c):
        c.wait()

    @pl.when(starts_group)
    def _():
        @pl.when((j == 0) & (i == 0))
        def _():
            for_block(te_ref[i], j, slot, start)

        for_block(te_ref[i], j, slot, wait)
        next_expert = nexte_ref[i]

        @pl.when(next_expert >= 0)
        def _():
            for_block(next_expert, j, 1 - slot, start)

        @pl.when((next_expert < 0) & (j + 1 < nj))
        def _():
            for_block(te_ref[0], j + 1, 1 - slot, start)

    @pl.when(valid)
    def _():
        x = x_ref[...]
        accs = [jnp.dot(x, wb[slot].astype(BF16), preferred_element_type=F32) for wb in wbuf]
        epilogue(accs, extra_refs, o_ref, j)

    @pl.when(jnp.logical_not(valid))
    def _():
        o_ref[...] = jnp.zeros(o_ref.shape, o_ref.dtype)


def gmm(x, weights, tables, *, tm, tn, epilogue, out_dtype, extras=(), extra_specs=(), x_col_block=0,
        name="gmm"):
    p = x.shape[0]
    _, k, n = weights[0].shape
    nj = pl.cdiv(n, tn)
    n_w = len(weights)
    kern = functools.partial(_gmm_kernel, n_w=n_w, n_extra=len(extras), epilogue=epilogue, tn=tn,
                             last_width=n - (nj - 1) * tn)
    return pl.pallas_call(
        kern,
        grid_spec=pltpu.PrefetchScalarGridSpec(
            num_scalar_prefetch=len(tables),
            grid=(nj, p // tm),
            in_specs=[pl.BlockSpec((tm, k), lambda j, i, te, tv, grp, ne, ng, xrow: (xrow[i], x_col_block))]
                     + [pl.BlockSpec(memory_space=pl.ANY)] * n_w + list(extra_specs),
            out_specs=pl.BlockSpec((tm, tn), lambda j, i, *_: (i, j)),
            scratch_shapes=[pltpu.VMEM((2, k, tn), F32)] * n_w + [pltpu.SemaphoreType.DMA((2, n_w))],
        ),
        out_shape=jax.ShapeDtypeStruct((p, n), out_dtype),
        compiler_params=_params("arbitrary", "arbitrary"),
        name=name,
    )(*tables, x, *weights, *extras)


def _dense_tables(m, tm, expert=0):
    n = m // tm
    zeros = jnp.zeros((n,), jnp.int32)
    return (jnp.full((n,), expert, jnp.int32), jnp.ones((n,), jnp.int32), zeros, zeros - 1,
            jnp.ones((1,), jnp.int32), jnp.arange(n, dtype=jnp.int32))


def _tile_spec(tm, tn):
    return pl.BlockSpec((tm, tn), lambda j, i, *_: (i, j))


def _ep_gelu_cols(first_gelu_block, accs, extra_refs, o_ref, j):
    del extra_refs
    acc = accs[0]

    @pl.when(j < first_gelu_block)
    def _():
        o_ref[...] = acc.astype(o_ref.dtype)

    @pl.when(j >= first_gelu_block)
    def _():
        o_ref[...] = jax.nn.gelu(acc).astype(o_ref.dtype)


def _ep_residual(accs, extra_refs, o_ref, j):
    del j
    o_ref[...] = (extra_refs[0][...] + accs[0]).astype(o_ref.dtype)


def _ep_swiglu(accs, extra_refs, o_ref, j):
    del extra_refs, j
    a, g = accs
    o_ref[...] = (jax.nn.silu(a) * g).astype(o_ref.dtype)


def _ep_plain(accs, extra_refs, o_ref, j):
    del extra_refs, j
    o_ref[...] = accs[0].astype(o_ref.dtype)


def _ep_qkv(q_blocks, k_blocks, accs, extra_refs, o_ref, j):
    acc = accs[0]
    qg_ref, kg_ref = extra_refs
    tn = acc.shape[1]

    @pl.when(j < q_blocks + k_blocks)
    def _():
        gain = jnp.where(j < q_blocks, qg_ref[...], kg_ref[...])
        for c in range(tn // SB_HEAD_DIM):
            sl = slice(c * SB_HEAD_DIM, (c + 1) * SB_HEAD_DIM)
            a = acc[:, sl]
            ms = jnp.mean(a * a, axis=-1, keepdims=True)
            o_ref[:, sl] = (a * lax.rsqrt(ms + EPS) * gain).astype(o_ref.dtype)

    @pl.when(j >= q_blocks + k_blocks)
    def _():
        o_ref[...] = acc.astype(o_ref.dtype)


def _mixer_kernel(za_ref, halo_ref, u_ref, v_ref, pw_ref, ps_ref, sgn_ref, sgw_ref, sgb_ref,
                  o_ref, zbuf, *, tiles_per_seq):
    i = pl.program_id(0)
    ts, pool_width = za_ref.shape
    sg_width = v_ref.shape[1]
    group_width = pool_width // len(POOL_WINDOWS)
    head_dim = sg_width // SG_HEADS
    tile_in_seq = i % tiles_per_seq

    za = za_ref[...]
    zbuf[POOL_HALO:POOL_HALO + ts, :] = za
    zbuf[0:POOL_HALO, :] = jnp.where(tile_in_seq == 0, 0.0, halo_ref[...])
    n_avail = (lax.broadcasted_iota(jnp.int32, (ts, 1), 0) + tile_in_seq * ts + 1).astype(F32)
    for g, w in enumerate(POOL_WINDOWS):
        cols = slice(g * group_width, (g + 1) * group_width)
        win = za[:, cols]
        for back in range(1, w):
            win = win + zbuf[POOL_HALO - back:POOL_HALO - back + ts, cols]
        mean = win / jnp.minimum(n_avail, float(w))
        pooled = (mean - za[:, cols]).astype(BF16)
        ya = jnp.dot(pooled, pw_ref[g], preferred_element_type=F32) * ps_ref[:, cols]
        o_ref[:, cols] = ya.astype(o_ref.dtype)

    v = v_ref[...]
    ms = jnp.mean(v * v, axis=-1, keepdims=True)
    vn = (v * lax.rsqrt(ms + EPS) * sgn_ref[...]).astype(BF16)
    pos_i = lax.broadcasted_iota(jnp.int32, (SG_LEN, SG_LEN), 0)
    pos_j = lax.broadcasted_iota(jnp.int32, (SG_LEN, SG_LEN), 1)
    allowed = (pos_j // CHUNK) <= (pos_i // CHUNK)
    for h in range(SG_HEADS):
        w_h = jnp.where(allowed, sgw_ref[h], 0.0).astype(BF16)
        hc = slice(h * head_dim, (h + 1) * head_dim)
        for c in range(ts // SG_LEN):
            rows = slice(c * SG_LEN, (c + 1) * SG_LEN)
            mixed = jnp.dot(w_h, vn[rows, hc], preferred_element_type=F32) + sgb_ref[:, hc]
            yb = u_ref[rows, hc] * mixed
            o_ref[rows, pool_width + h * head_dim:pool_width + (h + 1) * head_dim] = yb.astype(o_ref.dtype)


def mixer(z, pool_w, pool_scale, sg_norm, sg_w, sg_b, seq, ts=256):
    m = z.shape[0]
    pool_width = pool_scale.shape[0]
    sg_width = sg_norm.shape[0]
    assert pool_width == sg_width
    ts = min(ts, seq)
    tiles_per_seq = seq // ts
    halo_blocks_per_tile = ts // POOL_HALO
    head_dim = sg_width // SG_HEADS
    bias = jnp.repeat(sg_b.T, head_dim, axis=1)
    kern = functools.partial(_mixer_kernel, tiles_per_seq=tiles_per_seq)
    const2 = lambda i: (0, 0)
    const3 = lambda i: (0, 0, 0)
    return pl.pallas_call(
        kern,
        grid=(m // ts,),
        in_specs=[
            pl.BlockSpec((ts, pool_width), lambda i: (i, 0)),
            pl.BlockSpec((POOL_HALO, pool_width), lambda i: (jnp.maximum(i * halo_blocks_per_tile - 1, 0), 0)),
            pl.BlockSpec((ts, sg_width), lambda i: (i, 1)),
            pl.BlockSpec((ts, sg_width), lambda i: (i, 2)),
            pl.BlockSpec(pool_w.shape, const3),
            pl.BlockSpec((1, pool_width), const2),
            pl.BlockSpec((1, sg_width), const2),
            pl.BlockSpec(sg_w.shape, const3),
            pl.BlockSpec(bias.shape, const2),
        ],
        out_specs=pl.BlockSpec((ts, pool_width + sg_width), lambda i: (i, 0)),
        out_shape=jax.ShapeDtypeStruct((m, pool_width + sg_width), BF16),
        scratch_shapes=[pltpu.VMEM((ts + POOL_HALO, pool_width), F32)],
        compiler_params=_params("arbitrary"),
        name="mixer",
    )(z, z, z, z, pool_w.astype(BF16), pool_scale.reshape(1, -1), sg_norm.reshape(1, -1), sg_w, bias)


def _attn_kernel(q_ref, k_ref, v_ref, o_ref, *, group):
    qi = pl.program_id(2)
    tq = q_ref.shape[0]
    t_pos = lax.broadcasted_iota(jnp.int32, (tq, tq), 0)
    s_pos = lax.broadcasted_iota(jnp.int32, (tq, tq), 1)
    strict = s_pos < t_pos
    neg_tri = jnp.where(strict, -1.0, 0.0).astype(BF16)

    def head(g):
        return slice(g * SB_HEAD_DIM, (g + 1) * SB_HEAD_DIM)

    def sweep(k0, accs, carries, mask):
        zs, softpluses = [], []
        for g in range(group):
            z = lax.dot_general(q_ref[:, head(g)], k_ref[pl.ds(k0, tq), head(g)],
                                (((1,), (1,)), ((), ())), preferred_element_type=F32)
            neg_abs = lax.bitcast_convert_type(
                lax.bitcast_convert_type(z, jnp.uint32) | jnp.uint32(0x80000000), F32)
            softplus = jnp.maximum(z, 0.0) + jnp.log(1.0 + jnp.exp2(neg_abs)) * LOG2E
            if mask is not None:
                softplus = jnp.where(mask, softplus, 0.0)
            zs.append(z)
            softpluses.append(softplus)
        within = jnp.dot(jnp.concatenate([sp.astype(BF16) for sp in softpluses], axis=0), neg_tri,
                         preferred_element_type=F32)
        new_accs, new_carries = [], []
        for g in range(group):
            after = within[g * tq:(g + 1) * tq] + carries[g]
            attn = jnp.exp2(zs[g] - softpluses[g] + after)
            if mask is not None:
                attn = jnp.where(mask, attn, 0.0)
            out = jnp.dot(attn.astype(BF16), v_ref[pl.ds(k0, tq), head(g)], preferred_element_type=F32)
            new_accs.append(out if accs is None else accs[g] + out)
            new_carries.append(carries[g] - jnp.sum(softpluses[g], axis=1, keepdims=True))
        return tuple(new_accs), tuple(new_carries)

    def alive(carries):
        top = functools.reduce(jnp.maximum, carries)
        return (jnp.max(top) > SB_EXIT_LOG2).astype(jnp.int32)

    zero = jnp.zeros((tq, 1), F32)
    accs, carries = sweep(pl.multiple_of(qi * tq, tq), None, (zero,) * group, strict)

    def cond(state):
        n, live = state[0], state[1]
        return jnp.logical_and(n < qi, live > 0)

    def body(state):
        n, _, accs, carries = state
        accs, carries = sweep(pl.multiple_of((qi - 1 - n) * tq, tq), accs, carries, None)
        return n + 1, alive(carries), accs, carries

    _, _, accs, _ = lax.while_loop(cond, body, (jnp.int32(0), alive(carries), accs, carries))
    for g in range(group):
        o_ref[:, head(g)] = accs[g].astype(o_ref.dtype)


def sb_attention(qkv, batch, seq, heads, tq=256, group=8):
    m = qkv.shape[0]
    tq = min(tq, seq)
    group = min(group, heads)
    nq = seq // tq
    n_groups = heads // group
    width = group * SB_HEAD_DIM
    kern = functools.partial(_attn_kernel, group=group)
    return pl.pallas_call(
        kern,
        grid=(batch, n_groups, nq),
        in_specs=[
            pl.BlockSpec((tq, width), lambda b, h, i: (b * nq + i, h)),
            pl.BlockSpec((seq, width), lambda b, h, i: (b, n_groups + h)),
            pl.BlockSpec((seq, width), lambda b, h, i: (b, 2 * n_groups + h)),
        ],
        out_specs=pl.BlockSpec((tq, width), lambda b, h, i: (b * nq + i, h)),
        out_shape=jax.ShapeDtypeStruct((m, heads * SB_HEAD_DIM), BF16),
        compiler_params=_params("arbitrary", "arbitrary", "arbitrary"),
        name="sb_attention",
    )(qkv, qkv, qkv)


def _router_kernel(x_ref, g_ref, rt_ref, idx_ref, gate_ref, hp_ref):
    x = x_ref[...]
    tm, d = x.shape
    ms = jnp.mean(x * x, axis=-1, keepdims=True)
    h = x * lax.rsqrt(ms + EPS) * g_ref[...]
    logits = lax.dot_general(rt_ref[...], h, (((1,), (1,)), ((), ())),
                             precision=lax.Precision.HIGHEST, preferred_element_type=F32)
    n_exp = logits.shape[0]
    eid = lax.broadcasted_iota(jnp.int32, logits.shape, 0)
    m1 = jnp.max(logits, axis=0, keepdims=True)
    i1 = jnp.min(jnp.where(logits == m1, eid, n_exp), axis=0, keepdims=True)
    rest = jnp.where(eid == i1, -jnp.inf, logits)
    m2 = jnp.max(rest, axis=0, keepdims=True)
    i2 = jnp.min(jnp.where(rest == m2, eid, n_exp), axis=0, keepdims=True)
    e2 = jnp.exp(m2 - m1)
    g1 = 1.0 / (1.0 + e2)
    idx_ref[...] = jnp.concatenate([i1, i2], axis=0)
    gate_ref[...] = jnp.concatenate([g1, e2 * g1], axis=0)

    half = d // 2
    chunks = half // LANES
    lo = lax.bitcast_convert_type(h[:, :half].astype(BF16).astype(F32), jnp.uint32)
    hi = lax.bitcast_convert_type(h[:, half:].astype(BF16).astype(F32), jnp.uint32)
    words = hi | (lo >> 16)
    for c in range(chunks):
        hp_ref[pl.ds(c, tm, stride=chunks), :] = words[:, c * LANES:(c + 1) * LANES]


def router_top2(x, gain, router, tm=256):
    m, d = x.shape
    n_exp = router.shape[1]
    tm = min(tm, m)
    chunks = d // 2 // LANES
    return pl.pallas_call(
        _router_kernel,
        grid=(m // tm,),
        in_specs=[pl.BlockSpec((tm, d), lambda i: (i, 0)),
                  pl.BlockSpec((1, d), lambda i: (0, 0)),
                  pl.BlockSpec((n_exp, d), lambda i: (0, 0))],
        out_specs=[pl.BlockSpec((TOP_K, tm), lambda i: (0, i)),
                   pl.BlockSpec((TOP_K, tm), lambda i: (0, i)),
                   pl.BlockSpec((tm * chunks, LANES), lambda i: (i, 0))],
        out_shape=[jax.ShapeDtypeStruct((TOP_K, m), jnp.int32),
                   jax.ShapeDtypeStruct((TOP_K, m), F32),
                   jax.ShapeDtypeStruct((m * chunks, LANES), jnp.uint32)],
        compiler_params=_params("arbitrary"),
        name="router_top2",
    )(x, gain.reshape(1, d), router.T)


ISSUE_UNROLL = 8


def _row_start(row, height):
    if height == 1 or isinstance(row, int):
        return row * height
    return pl.multiple_of(row * height, height)


def _row_copy(src_hbm, src_row, dst_vmem, dst_row, sem, height):
    return pltpu.make_async_copy(src_hbm.at[pl.ds(_row_start(src_row, height), height), :],
                                 dst_vmem.at[pl.ds(_row_start(dst_row, height), height), :], sem)


def _issue_rows(src_hbm, index_of_row, dst_vmem, sem, height=1):
    def group(b, c):
        for u in range(ISSUE_UNROLL):
            r = b * ISSUE_UNROLL + u
            _row_copy(src_hbm, index_of_row(r), dst_vmem, r, sem, height).start(priority=u % 2)
        return c
    lax.fori_loop(0, dst_vmem.shape[0] // height // ISSUE_UNROLL, group, 0)


def _wait_rows(src_hbm, dst_vmem, sem, height=1):
    def one(r, c):
        _row_copy(src_hbm, 0, dst_vmem, r, sem, height).wait()
        return c
    lax.fori_loop(0, dst_vmem.shape[0] // height, one, 0, unroll=ISSUE_UNROLL)


def _unpack_rows(x_ref, tm, chunks):
    words = jnp.concatenate([x_ref[pl.ds(c, tm, stride=chunks), :] for c in range(chunks)], axis=1)
    lo = lax.bitcast_convert_type(words << 16, F32).astype(BF16)
    hi = lax.bitcast_convert_type(words & jnp.uint32(0xFFFF0000), F32).astype(BF16)
    return jnp.concatenate([lo, hi], axis=1)


def _gather_kernel(src_ref, x_hbm, o_ref, buf, sems, *, height):
    i = pl.program_id(0)
    rows = o_ref.shape[0]
    slot = i % 2

    def issue(tile, slot):
        _issue_rows(x_hbm, lambda r: src_ref[tile * rows + r], buf.at[slot], sems.at[slot], height)

    @pl.when(i == 0)
    def _():
        issue(0, 0)

    @pl.when(i + 1 < pl.num_programs(0))
    def _():
        issue(i + 1, 1 - slot)

    _wait_rows(x_hbm, buf.at[slot], sems.at[slot], height)
    o_ref[...] = _unpack_rows(buf.at[slot], rows, height)


def gather_rows(x, src_rows, height, rows=256):
    p = src_rows.shape[0]
    kern = functools.partial(_gather_kernel, height=height)
    return pl.pallas_call(
        kern,
        grid_spec=pltpu.PrefetchScalarGridSpec(
            num_scalar_prefetch=1,
            grid=(p // rows,),
            in_specs=[pl.BlockSpec(memory_space=pl.ANY)],
            out_specs=pl.BlockSpec((rows, 2 * LANES * height), lambda i, src: (i, 0)),
            scratch_shapes=[pltpu.VMEM((2, rows * height, LANES), x.dtype), pltpu.SemaphoreType.DMA((2,))],
        ),
        out_shape=jax.ShapeDtypeStruct((p, 2 * LANES * height), BF16),
        compiler_params=_params("arbitrary"),
        name="gather_rows",
    )(src_rows, x)


def _combine_kernel(pos_ref, x_ref, gate_ref, y_hbm, o_ref, buf, sems):
    i = pl.program_id(0)
    rows = x_ref.shape[0]
    n_tokens = pos_ref.shape[0] // TOP_K
    slot = i % 2

    def issue(tile, slot):
        for k in range(TOP_K):
            _issue_rows(y_hbm, lambda r: pos_ref[k * n_tokens + tile * rows + r],
                        buf.at[slot, k], sems.at[slot])

    @pl.when(i == 0)
    def _():
        issue(0, 0)

    @pl.when(i + 1 < pl.num_programs(0))
    def _():
        issue(i + 1, 1 - slot)

    for k in range(TOP_K):
        _wait_rows(y_hbm, buf.at[slot, k], sems.at[slot])
    out = x_ref[...]
    for k in range(TOP_K):
        out = out + gate_ref[:, k:k + 1] * buf[slot, k]
    o_ref[...] = out


def combine(x, y, pos, gate, rows=128):
    m, d = x.shape
    rows = min(rows, m)
    return pl.pallas_call(
        _combine_kernel,
        grid_spec=pltpu.PrefetchScalarGridSpec(
            num_scalar_prefetch=1,
            grid=(m // rows,),
            in_specs=[pl.BlockSpec((rows, d), lambda i, pos: (i, 0)),
                      pl.BlockSpec((rows, TOP_K), lambda i, pos: (i, 0)),
                      pl.BlockSpec(memory_space=pl.ANY)],
            out_specs=pl.BlockSpec((rows, d), lambda i, pos: (i, 0)),
            scratch_shapes=[pltpu.VMEM((2, TOP_K, rows, d), F32), pltpu.SemaphoreType.DMA((2,))],
        ),
        out_shape=jax.ShapeDtypeStruct((m, d), F32),
        compiler_params=_params("arbitrary"),
        name="combine",
    )(pos, x, gate.T, y)


def _routing_tables(idx, n_exp, tm):
    m = idx.shape[1]
    n_assign = TOP_K * m
    n_tiles = n_assign // tm + n_exp
    p = n_tiles * tm
    e_flat = idx.reshape(-1)
    onehot = (e_flat[:, None] == jnp.arange(n_exp)[None, :]).astype(jnp.int32)
    before = jnp.cumsum(onehot, axis=0) - onehot
    rank = jnp.sum(before * onehot, axis=1)
    counts = jnp.sum(onehot, axis=0)
    padded = ((counts + tm - 1) // tm) * tm
    ends = jnp.cumsum(padded)
    offsets = ends - padded
    pos = offsets[e_flat] + rank
    token = jnp.tile(jnp.arange(m, dtype=jnp.int32), TOP_K)
    src_rows = jnp.zeros((p,), jnp.int32).at[pos].set(token)
    tile_start = jnp.arange(n_tiles, dtype=jnp.int32) * tm
    tile_expert = jnp.sum((tile_start[:, None] >= ends[None, :]).astype(jnp.int32), axis=1)
    tile_valid = (tile_start < ends[-1]).astype(jnp.int32)
    tile_expert = jnp.minimum(tile_expert, n_exp - 1).astype(jnp.int32)
    experts = jnp.arange(n_exp, dtype=jnp.int32)
    present = counts > 0
    later_present = (experts[None, :] > experts[:, None]) & present[None, :]
    next_present = jnp.min(jnp.where(later_present, experts[None, :], n_exp), axis=1)
    next_present = jnp.where(next_present == n_exp, -1, next_present).astype(jnp.int32)
    group_of_expert = (jnp.cumsum(present.astype(jnp.int32)) - 1).astype(jnp.int32)
    n_groups = jnp.sum(present.astype(jnp.int32)).reshape(1)
    tile_index = jnp.arange(n_tiles, dtype=jnp.int32)
    x_row_block = jnp.minimum(tile_index, jnp.sum(tile_valid) - 1).astype(jnp.int32)
    tables = (tile_expert, tile_valid, group_of_expert[tile_expert], next_present[tile_expert], n_groups,
              x_row_block)
    return src_rows, pos.astype(jnp.int32), tables


def _even_layer(x, seq, norm1, w_in, pool_w, pool_scale, sg_norm, sg_w, sg_b, w_out, norm2, w1, w3, w2,
                tm=512, tm_wide=1024, tn=512, tn_up=256):
    m, d = x.shape
    pool_width = pool_scale.shape[0]
    tm_wide = min(tm_wide, m)
    wide = _dense_tables(m, tm_wide)
    h = rmsnorm(x, norm1)
    z = gmm(h, [w_in[None]], wide, tm=tm_wide, tn=tn, out_dtype=F32,
            epilogue=functools.partial(_ep_gelu_cols, pool_width // tn), name="w_in")
    y = mixer(z, pool_w, pool_scale, sg_norm, sg_w, sg_b, seq)
    x = gmm(y, [w_out[None]], wide, tm=tm_wide, tn=tn, out_dtype=F32, epilogue=_ep_residual,
            extras=[x], extra_specs=[_tile_spec(tm_wide, tn)], name="e_w_out")
    h = rmsnorm(x, norm2)
    hf = gmm(h, [w1[None], w3[None]], wide, tm=tm_wide, tn=tn_up, out_dtype=BF16, epilogue=_ep_swiglu,
             name="ffn_up")
    d_ff = w2.shape[0]
    w2h = w2.reshape(2, d_ff // 2, d)
    for half in range(2):
        x = gmm(hf, [w2h], _dense_tables(m, tm, half), tm=tm, tn=tn, out_dtype=F32, epilogue=_ep_residual,
                extras=[x], extra_specs=[_tile_spec(tm, tn)], x_col_block=half, name="ffn_down")
    return x


def _odd_layer(x, batch, seq, norm1, w_qkv, q_norm, k_norm, w_out, norm2, router, w1, w3, w2,
               tm=512, tm_wide=1024, tn=512):
    m, d = x.shape
    heads = d // SB_HEAD_DIM
    tm_wide = min(tm_wide, m)
    wide = _dense_tables(m, tm_wide)
    h = rmsnorm(x, norm1)
    gain_spec = pl.BlockSpec((1, SB_HEAD_DIM), lambda j, i, *_: (0, 0))
    q_gain = q_norm * (SB_HEAD_DIM ** -0.5 * LOG2E)
    qkv = gmm(h, [w_qkv[None]], wide, tm=tm_wide, tn=tn, out_dtype=BF16,
              epilogue=functools.partial(_ep_qkv, d // tn, d // tn),
              extras=[q_gain.reshape(1, -1), k_norm.reshape(1, -1)], extra_specs=[gain_spec, gain_spec],
              name="w_qkv")
    o = sb_attention(qkv, batch, seq, heads)
    x = gmm(o, [w_out[None]], wide, tm=tm_wide, tn=tn, out_dtype=F32, epilogue=_ep_residual,
            extras=[x], extra_specs=[_tile_spec(tm_wide, tn)], name="o_w_out")

    n_exp = router.shape[1]
    idx, gate, h_packed = router_top2(x, norm2, router)
    src_rows, pos, tables = _routing_tables(idx, n_exp, tm)
    chunks = d // 2 // LANES
    xs = gather_rows(h_packed, src_rows, chunks)
    hs = gmm(xs, [w1, w3], tables, tm=tm, tn=tn, out_dtype=BF16, epilogue=_ep_swiglu, name="moe_up")
    ys = gmm(hs, [w2], tables, tm=tm, tn=tn, out_dtype=F32, epilogue=_ep_plain, name="moe_down")
    return combine(x, ys, pos, gate)


def kernel(x, e_norm1, e_w_in, e_pool_w, e_pool_scale, e_sg_norm, e_sg_w, e_sg_b, e_w_out, e_norm2, e_ffn_w1, e_ffn_w3, e_ffn_w2, o_norm1, o_w_qkv, o_q_norm, o_k_norm, o_w_out, o_norm2, o_router, o_moe_w1, o_moe_w3, o_moe_w2):
    batch, seq, d = x.shape
    depth = e_norm1.shape[0] + o_norm1.shape[0]
    xf = x.reshape(batch * seq, d)
    for layer in range(depth):
        i = layer // 2
        if layer % 2 == 0:
            xf = _even_layer(xf, seq, e_norm1[i], e_w_in[i], e_pool_w[i], e_pool_scale[i], e_sg_norm[i],
                             e_sg_w[i], e_sg_b[i], e_w_out[i], e_norm2[i], e_ffn_w1[i], e_ffn_w3[i], e_ffn_w2[i])
        else:
            xf = _odd_layer(xf, batch, seq, o_norm1[i], o_w_qkv[i], o_q_norm[i], o_k_norm[i], o_w_out[i],
                            o_norm2[i], o_router[i], o_moe_w1[i], o_moe_w3[i], o_moe_w2[i])
    return xf.reshape(batch, seq, d)
```

```python
import functools

import jax
import jax.numpy as jnp
from jax import lax
from jax.experimental import pallas as pl
from jax.experimental.pallas import tpu as pltpu

EPS = 1e-6
CHUNK = 64
SG_LEN = 128
SG_HEADS = 8
POOL_WINDOWS = (2, 4, 8, 16)
POOL_HALO = 16
SB_HEAD_DIM = 128
SB_EXIT_LOG2 = -150.0
LOG2E = 1.4426950408889634
TOP_K = 2

V7X_VMEM_BYTES = 64 * 1024 * 1024
VMEM_LIMIT_BYTES = V7X_VMEM_BYTES - 8 * 1024 * 1024
LANES = 128

F32 = jnp.float32
BF16 = jnp.bfloat16


def _params(*semantics):
    return pltpu.CompilerParams(dimension_semantics=semantics, vmem_limit_bytes=VMEM_LIMIT_BYTES)


def _rmsnorm_kernel(x_ref, g_ref, o_ref):
    x = x_ref[...]
    ms = jnp.mean(x * x, axis=-1, keepdims=True)
    o_ref[...] = (x * lax.rsqrt(ms + EPS) * g_ref[...]).astype(o_ref.dtype)


def rmsnorm(x, gain, tm=256):
    m, d = x.shape
    tm = min(tm, m)
    return pl.pallas_call(
        _rmsnorm_kernel,
        grid=(m // tm,),
        in_specs=[pl.BlockSpec((tm, d), lambda i: (i, 0)),
                  pl.BlockSpec((1, d), lambda i: (0, 0))],
        out_specs=pl.BlockSpec((tm, d), lambda i: (i, 0)),
        out_shape=jax.ShapeDtypeStruct((m, d), BF16),
        compiler_params=_params("arbitrary"),
        name="rmsnorm",
    )(x, gain.reshape(1, d))


def _gmm_kernel(te_ref, rows_ref, grp_ref, nexte_ref, ngroups_ref, xrow_ref, x_ref, *refs,
                n_w, n_extra, epilogue, tn, last_width, row_options):
    del xrow_ref
    w_hbm = refs[:n_w]
    extra_refs = refs[n_w:n_w + n_extra]
    o_ref = refs[n_w + n_extra]
    wbuf = refs[n_w + n_extra + 1:2 * n_w + n_extra + 1]
    sems = refs[2 * n_w + n_extra + 1]
    j = pl.program_id(0)
    i = pl.program_id(1)
    nj = pl.num_programs(0)
    valid = rows_ref[i] != 0
    starts_group = valid & ((i == 0) | (te_ref[i] != te_ref[jnp.maximum(i - 1, 0)]))
    slot = (j * ngroups_ref[0] + grp_ref[i]) % 2

    def block_copies(expert, jj, slot, width):
        col0 = pl.multiple_of(jj * tn, tn)
        return [pltpu.make_async_copy(w_hbm[k].at[expert, :, pl.ds(col0, width)],
                                      wbuf[k].at[slot, :, pl.ds(0, width)], sems.at[slot, k])
                for k in range(n_w)]

    def for_block(expert, jj, slot, action):
        if last_width == tn:
            for c in block_copies(expert, jj, slot, tn):
                action(c)
        else:
            @pl.when(jj < nj - 1)
            def _():
                for c in block_copies(expert, jj, slot, tn):
                    action(c)

            @pl.when(jj == nj - 1)
            def _():
                for c in block_copies(expert, jj, slot, last_width):
                    action(c)

    def start(c):
        c.start()

    def wait(c):
        c.wait()

    @pl.when(starts_group)
    def _():
        @pl.when((j == 0) & (i == 0))
        def _():
            for_block(te_ref[i], j, slot, start)

        for_block(te_ref[i], j, slot, wait)
        next_expert = nexte_ref[i]

        @pl.when(next_expert >= 0)
        def _():
            for_block(next_expert, j, 1 - slot, start)

        @pl.when((next_expert < 0) & (j + 1 < nj))
        def _():
            for_block(te_ref[0], j + 1, 1 - slot, start)

    tm = x_ref.shape[0]
    for rows in row_options:
        @pl.when(rows_ref[i] == rows)
        def _(rows=rows):
            x = x_ref[...] if rows == tm else x_ref[:rows, :]
            accs = [jnp.dot(x, wb[slot].astype(BF16), preferred_element_type=F32) for wb in wbuf]
            if rows == tm:
                epilogue(accs, extra_refs, o_ref, j)
            else:
                epilogue(accs, extra_refs, o_ref.at[pl.ds(0, rows), :], j)
                o_ref[rows:, :] = jnp.zeros((tm - rows, o_ref.shape[1]), o_ref.dtype)

    @pl.when(jnp.logical_not(valid))
    def _():
        o_ref[...] = jnp.zeros(o_ref.shape, o_ref.dtype)


def gmm(x, weights, tables, *, tm, tn, epilogue, out_dtype, extras=(), extra_specs=(), x_col_block=0,
        row_options=None, name="gmm"):
    p = x.shape[0]
    _, k, n = weights[0].shape
    nj = pl.cdiv(n, tn)
    n_w = len(weights)
    row_options = (tm,) if row_options is None else tuple(row_options)
    assert not extras or row_options == (tm,)
    kern = functools.partial(_gmm_kernel, n_w=n_w, n_extra=len(extras), epilogue=epilogue, tn=tn,
                             last_width=n - (nj - 1) * tn, row_options=row_options)
    return pl.pallas_call(
        kern,
        grid_spec=pltpu.PrefetchScalarGridSpec(
            num_scalar_prefetch=len(tables),
            grid=(nj, p // tm),
            in_specs=[pl.BlockSpec((tm, k), lambda j, i, te, tv, grp, ne, ng, xrow: (xrow[i], x_col_block))]
                     + [pl.BlockSpec(memory_space=pl.ANY)] * n_w + list(extra_specs),
            out_specs=pl.BlockSpec((tm, tn), lambda j, i, *_: (i, j)),
            scratch_shapes=[pltpu.VMEM((2, k, tn), F32)] * n_w + [pltpu.SemaphoreType.DMA((2, n_w))],
        ),
        out_shape=jax.ShapeDtypeStruct((p, n), out_dtype),
        compiler_params=_params("arbitrary", "arbitrary"),
        name=name,
    )(*tables, x, *weights, *extras)


def _dense_tables(m, tm, expert=0):
    n = m // tm
    zeros = jnp.zeros((n,), jnp.int32)
    return (jnp.full((n,), expert, jnp.int32), jnp.full((n,), tm, jnp.int32), zeros, zeros - 1,
            jnp.ones((1,), jnp.int32), jnp.arange(n, dtype=jnp.int32))


def _tile_spec(tm, tn):
    return pl.BlockSpec((tm, tn), lambda j, i, *_: (i, j))


def _ep_gelu_cols(first_gelu_block, accs, extra_refs, o_ref, j):
    del extra_refs
    acc = accs[0]

    @pl.when(j < first_gelu_block)
    def _():
        o_ref[...] = acc.astype(o_ref.dtype)

    @pl.when(j >= first_gelu_block)
    def _():
        o_ref[...] = jax.nn.gelu(acc).astype(o_ref.dtype)


def _ep_residual(accs, extra_refs, o_ref, j):
    del j
    o_ref[...] = (extra_refs[0][...] + accs[0]).astype(o_ref.dtype)


def _ep_swiglu(accs, extra_refs, o_ref, j):
    del extra_refs, j
    a, g = accs
    o_ref[...] = (jax.nn.silu(a) * g).astype(o_ref.dtype)


def _ep_plain(accs, extra_refs, o_ref, j):
    del extra_refs, j
    o_ref[...] = accs[0].astype(o_ref.dtype)


def _ep_qkv(q_blocks, k_blocks, accs, extra_refs, o_ref, j):
    acc = accs[0]
    qg_ref, kg_ref = extra_refs
    tn = acc.shape[1]

    @pl.when(j < q_blocks + k_blocks)
    def _():
        gain = jnp.where(j < q_blocks, qg_ref[...], kg_ref[...])
        for c in range(tn // SB_HEAD_DIM):
            sl = slice(c * SB_HEAD_DIM, (c + 1) * SB_HEAD_DIM)
            a = acc[:, sl]
            ms = jnp.mean(a * a, axis=-1, keepdims=True)
            o_ref[:, sl] = (a * lax.rsqrt(ms + EPS) * gain).astype(o_ref.dtype)

    @pl.when(j >= q_blocks + k_blocks)
    def _():
        o_ref[...] = acc.astype(o_ref.dtype)


def _mixer_kernel(za_ref, halo_ref, u_ref, v_ref, pw_ref, ps_ref, sgn_ref, sgw_ref, sgb_ref,
                  o_ref, zbuf, *, tiles_per_seq):
    i = pl.program_id(0)
    ts, pool_width = za_ref.shape
    sg_width = v_ref.shape[1]
    group_width = pool_width // len(POOL_WINDOWS)
    head_dim = sg_width // SG_HEADS
    tile_in_seq = i % tiles_per_seq

    za = za_ref[...]
    zbuf[POOL_HALO:POOL_HALO + ts, :] = za
    zbuf[0:POOL_HALO, :] = jnp.where(tile_in_seq == 0, 0.0, halo_ref[...])
    n_avail = (lax.broadcasted_iota(jnp.int32, (ts, 1), 0) + tile_in_seq * ts + 1).astype(F32)
    for g, w in enumerate(POOL_WINDOWS):
        cols = slice(g * group_width, (g + 1) * group_width)
        win = za[:, cols]
        for back in range(1, w):
            win = win + zbuf[POOL_HALO - back:POOL_HALO - back + ts, cols]
        mean = win / jnp.minimum(n_avail, float(w))
        pooled = (mean - za[:, cols]).astype(BF16)
        ya = jnp.dot(pooled, pw_ref[g], preferred_element_type=F32) * ps_ref[:, cols]
        o_ref[:, cols] = ya.astype(o_ref.dtype)

    v = v_ref[...]
    ms = jnp.mean(v * v, axis=-1, keepdims=True)
    vn = (v * lax.rsqrt(ms + EPS) * sgn_ref[...]).astype(BF16)
    pos_i = lax.broadcasted_iota(jnp.int32, (SG_LEN, SG_LEN), 0)
    pos_j = lax.broadcasted_iota(jnp.int32, (SG_LEN, SG_LEN), 1)
    allowed = (pos_j // CHUNK) <= (pos_i // CHUNK)
    for h in range(SG_HEADS):
        w_h = jnp.where(allowed, sgw_ref[h], 0.0).astype(BF16)
        hc = slice(h * head_dim, (h + 1) * head_dim)
        for c in range(ts // SG_LEN):
            rows = slice(c * SG_LEN, (c + 1) * SG_LEN)
            mixed = jnp.dot(w_h, vn[rows, hc], preferred_element_type=F32) + sgb_ref[:, hc]
            yb = u_ref[rows, hc] * mixed
            o_ref[rows, pool_width + h * head_dim:pool_width + (h + 1) * head_dim] = yb.astype(o_ref.dtype)


def mixer(z, pool_w, pool_scale, sg_norm, sg_w, sg_b, seq, ts=256):
    m = z.shape[0]
    pool_width = pool_scale.shape[0]
    sg_width = sg_norm.shape[0]
    assert pool_width == sg_width
    ts = min(ts, seq)
    tiles_per_seq = seq // ts
    halo_blocks_per_tile = ts // POOL_HALO
    head_dim = sg_width // SG_HEADS
    bias = jnp.repeat(sg_b.T, head_dim, axis=1)
    kern = functools.partial(_mixer_kernel, tiles_per_seq=tiles_per_seq)
    const2 = lambda i: (0, 0)
    const3 = lambda i: (0, 0, 0)
    return pl.pallas_call(
        kern,
        grid=(m // ts,),
        in_specs=[
            pl.BlockSpec((ts, pool_width), lambda i: (i, 0)),
            pl.BlockSpec((POOL_HALO, pool_width), lambda i: (jnp.maximum(i * halo_blocks_per_tile - 1, 0), 0)),
            pl.BlockSpec((ts, sg_width), lambda i: (i, 1)),
            pl.BlockSpec((ts, sg_width), lambda i: (i, 2)),
            pl.BlockSpec(pool_w.shape, const3),
            pl.BlockSpec((1, pool_width), const2),
            pl.BlockSpec((1, sg_width), const2),
            pl.BlockSpec(sg_w.shape, const3),
            pl.BlockSpec(bias.shape, const2),
        ],
        out_specs=pl.BlockSpec((ts, pool_width + sg_width), lambda i: (i, 0)),
        out_shape=jax.ShapeDtypeStruct((m, pool_width + sg_width), BF16),
        scratch_shapes=[pltpu.VMEM((ts + POOL_HALO, pool_width), F32)],
        compiler_params=_params("arbitrary"),
        name="mixer",
    )(z, z, z, z, pool_w.astype(BF16), pool_scale.reshape(1, -1), sg_norm.reshape(1, -1), sg_w, bias)


def _attn_kernel(q_ref, k_ref, v_ref, o_ref, *, group):
    qi = pl.program_id(2)
    tq = q_ref.shape[0]
    t_pos = lax.broadcasted_iota(jnp.int32, (tq, tq), 0)
    s_pos = lax.broadcasted_iota(jnp.int32, (tq, tq), 1)
    strict = s_pos < t_pos
    neg_tri = jnp.where(strict, -1.0, 0.0).astype(BF16)

    def head(g):
        return slice(g * SB_HEAD_DIM, (g + 1) * SB_HEAD_DIM)

    def sweep(k0, accs, carries, mask):
        zs, softpluses = [], []
        for g in range(group):
            z = lax.dot_general(q_ref[:, head(g)], k_ref[pl.ds(k0, tq), head(g)],
                                (((1,), (1,)), ((), ())), preferred_element_type=F32)
            neg_abs = lax.bitcast_convert_type(
                lax.bitcast_convert_type(z, jnp.uint32) | jnp.uint32(0x80000000), F32)
            softplus = jnp.maximum(z, 0.0) + jnp.log(1.0 + jnp.exp2(neg_abs)) * LOG2E
            if mask is not None:
                softplus = jnp.where(mask, softplus, 0.0)
            zs.append(z)
            softpluses.append(softplus)
        within = jnp.dot(jnp.concatenate([sp.astype(BF16) for sp in softpluses], axis=0), neg_tri,
                         preferred_element_type=F32)
        new_accs, new_carries = [], []
        for g in range(group):
            after = within[g * tq:(g + 1) * tq] + carries[g]
            attn = jnp.exp2(zs[g] - softpluses[g] + after)
            if mask is not None:
                attn = jnp.where(mask, attn, 0.0)
            out = jnp.dot(attn.astype(BF16), v_ref[pl.ds(k0, tq), head(g)], preferred_element_type=F32)
            new_accs.append(out if accs is None else accs[g] + out)
            new_carries.append(carries[g] - jnp.sum(softpluses[g], axis=1, keepdims=True))
        return tuple(new_accs), tuple(new_carries)

    def alive(carries):
        top = functools.reduce(jnp.maximum, carries)
        return (jnp.max(top) > SB_EXIT_LOG2).astype(jnp.int32)

    zero = jnp.zeros((tq, 1), F32)
    accs, carries = sweep(pl.multiple_of(qi * tq, tq), None, (zero,) * group, strict)

    def cond(state):
        n, live = state[0], state[1]
        return jnp.logical_and(n < qi, live > 0)

    def body(state):
        n, _, accs, carries = state
        accs, carries = sweep(pl.multiple_of((qi - 1 - n) * tq, tq), accs, carries, None)
        return n + 1, alive(carries), accs, carries

    _, _, accs, _ = lax.while_loop(cond, body, (jnp.int32(0), alive(carries), accs, carries))
    for g in range(group):
        o_ref[:, head(g)] = accs[g].astype(o_ref.dtype)


def sb_attention(qkv, batch, seq, heads, tq=256, group=8):
    m = qkv.shape[0]
    tq = min(tq, seq)
    group = min(group, heads)
    nq = seq // tq
    n_groups = heads // group
    width = group * SB_HEAD_DIM
    kern = functools.partial(_attn_kernel, group=group)
    return pl.pallas_call(
        kern,
        grid=(batch, n_groups, nq),
        in_specs=[
            pl.BlockSpec((tq, width), lambda b, h, i: (b * nq + i, h)),
            pl.BlockSpec((seq, width), lambda b, h, i: (b, n_groups + h)),
            pl.BlockSpec((seq, width), lambda b, h, i: (b, 2 * n_groups + h)),
        ],
        out_specs=pl.BlockSpec((tq, width), lambda b, h, i: (b * nq + i, h)),
        out_shape=jax.ShapeDtypeStruct((m, heads * SB_HEAD_DIM), BF16),
        compiler_params=_params("arbitrary", "arbitrary", "arbitrary"),
        name="sb_attention",
    )(qkv, qkv, qkv)


def _router_kernel(x_ref, g_ref, rt_ref, idx_ref, gate_ref, hp_ref):
    x = x_ref[...]
    tm, d = x.shape
    ms = jnp.mean(x * x, axis=-1, keepdims=True)
    h = x * lax.rsqrt(ms + EPS) * g_ref[...]
    logits = lax.dot_general(rt_ref[...], h, (((1,), (1,)), ((), ())),
                             precision=lax.Precision.HIGHEST, preferred_element_type=F32)
    n_exp = logits.shape[0]
    eid = lax.broadcasted_iota(jnp.int32, logits.shape, 0)
    m1 = jnp.max(logits, axis=0, keepdims=True)
    i1 = jnp.min(jnp.where(logits == m1, eid, n_exp), axis=0, keepdims=True)
    rest = jnp.where(eid == i1, -jnp.inf, logits)
    m2 = jnp.max(rest, axis=0, keepdims=True)
    i2 = jnp.min(jnp.where(rest == m2, eid, n_exp), axis=0, keepdims=True)
    e2 = jnp.exp(m2 - m1)
    g1 = 1.0 / (1.0 + e2)
    idx_ref[...] = jnp.concatenate([i1, i2], axis=0)
    gate_ref[...] = jnp.concatenate([g1, e2 * g1], axis=0)

    half = d // 2
    chunks = half // LANES
    lo = lax.bitcast_convert_type(h[:, :half].astype(BF16).astype(F32), jnp.uint32)
    hi = lax.bitcast_convert_type(h[:, half:].astype(BF16).astype(F32), jnp.uint32)
    words = hi | (lo >> 16)
    for c in range(chunks):
        hp_ref[pl.ds(c, tm, stride=chunks), :] = words[:, c * LANES:(c + 1) * LANES]


def router_top2(x, gain, router, tm=256):
    m, d = x.shape
    n_exp = router.shape[1]
    tm = min(tm, m)
    chunks = d // 2 // LANES
    return pl.pallas_call(
        _router_kernel,
        grid=(m // tm,),
        in_specs=[pl.BlockSpec((tm, d), lambda i: (i, 0)),
                  pl.BlockSpec((1, d), lambda i: (0, 0)),
                  pl.BlockSpec((n_exp, d), lambda i: (0, 0))],
        out_specs=[pl.BlockSpec((TOP_K, tm), lambda i: (0, i)),
                   pl.BlockSpec((TOP_K, tm), lambda i: (0, i)),
                   pl.BlockSpec((tm * chunks, LANES), lambda i: (i, 0))],
        out_shape=[jax.ShapeDtypeStruct((TOP_K, m), jnp.int32),
                   jax.ShapeDtypeStruct((TOP_K, m), F32),
                   jax.ShapeDtypeStruct((m * chunks, LANES), jnp.uint32)],
        compiler_params=_params("arbitrary"),
        name="router_top2",
    )(x, gain.reshape(1, d), router.T)


ISSUE_UNROLL = 8


def _row_start(row, height):
    if height == 1 or isinstance(row, int):
        return row * height
    return pl.multiple_of(row * height, height)


def _row_copy(src_hbm, src_row, dst_vmem, dst_row, sem, height):
    return pltpu.make_async_copy(src_hbm.at[pl.ds(_row_start(src_row, height), height), :],
                                 dst_vmem.at[pl.ds(_row_start(dst_row, height), height), :], sem)


def _issue_rows(src_hbm, index_of_row, dst_vmem, sem, height=1):
    def group(b, c):
        for u in range(ISSUE_UNROLL):
            r = b * ISSUE_UNROLL + u
            _row_copy(src_hbm, index_of_row(r), dst_vmem, r, sem, height).start(priority=u % 2)
        return c
    lax.fori_loop(0, dst_vmem.shape[0] // height // ISSUE_UNROLL, group, 0)


def _wait_rows(src_hbm, dst_vmem, sem, height=1):
    def one(r, c):
        _row_copy(src_hbm, 0, dst_vmem, r, sem, height).wait()
        return c
    lax.fori_loop(0, dst_vmem.shape[0] // height, one, 0, unroll=ISSUE_UNROLL)


def _unpack_rows(x_ref, tm, chunks):
    words = jnp.concatenate([x_ref[pl.ds(c, tm, stride=chunks), :] for c in range(chunks)], axis=1)
    lo = lax.bitcast_convert_type(words << 16, F32).astype(BF16)
    hi = lax.bitcast_convert_type(words & jnp.uint32(0xFFFF0000), F32).astype(BF16)
    return jnp.concatenate([lo, hi], axis=1)


def _gather_kernel(src_ref, x_hbm, o_ref, buf, sems, *, height):
    i = pl.program_id(0)
    rows = o_ref.shape[0]
    slot = i % 2

    def issue(tile, slot):
        _issue_rows(x_hbm, lambda r: src_ref[tile * rows + r], buf.at[slot], sems.at[slot], height)

    @pl.when(i == 0)
    def _():
        issue(0, 0)

    @pl.when(i + 1 < pl.num_programs(0))
    def _():
        issue(i + 1, 1 - slot)

    _wait_rows(x_hbm, buf.at[slot], sems.at[slot], height)
    o_ref[...] = _unpack_rows(buf.at[slot], rows, height)


def gather_rows(x, src_rows, height, rows=256):
    p = src_rows.shape[0]
    kern = functools.partial(_gather_kernel, height=height)
    return pl.pallas_call(
        kern,
        grid_spec=pltpu.PrefetchScalarGridSpec(
            num_scalar_prefetch=1,
            grid=(p // rows,),
            in_specs=[pl.BlockSpec(memory_space=pl.ANY)],
            out_specs=pl.BlockSpec((rows, 2 * LANES * height), lambda i, src: (i, 0)),
            scratch_shapes=[pltpu.VMEM((2, rows * height, LANES), x.dtype), pltpu.SemaphoreType.DMA((2,))],
        ),
        out_shape=jax.ShapeDtypeStruct((p, 2 * LANES * height), BF16),
        compiler_params=_params("arbitrary"),
        name="gather_rows",
    )(src_rows, x)


def _combine_kernel(pos_ref, x_ref, gate_ref, y_hbm, o_ref, buf, sems):
    i = pl.program_id(0)
    rows = x_ref.shape[0]
    n_tokens = pos_ref.shape[0] // TOP_K
    slot = i % 2

    def issue(tile, slot):
        for k in range(TOP_K):
            _issue_rows(y_hbm, lambda r: pos_ref[k * n_tokens + tile * rows + r],
                        buf.at[slot, k], sems.at[slot])

    @pl.when(i == 0)
    def _():
        issue(0, 0)

    @pl.when(i + 1 < pl.num_programs(0))
    def _():
        issue(i + 1, 1 - slot)

    for k in range(TOP_K):
        _wait_rows(y_hbm, buf.at[slot, k], sems.at[slot])
    out = x_ref[...]
    for k in range(TOP_K):
        out = out + gate_ref[:, k:k + 1] * buf[slot, k]
    o_ref[...] = out


def combine(x, y, pos, gate, rows=128):
    m, d = x.shape
    rows = min(rows, m)
    return pl.pallas_call(
        _combine_kernel,
        grid_spec=pltpu.PrefetchScalarGridSpec(
            num_scalar_prefetch=1,
            grid=(m // rows,),
            in_specs=[pl.BlockSpec((rows, d), lambda i, pos: (i, 0)),
                      pl.BlockSpec((rows, TOP_K), lambda i, pos: (i, 0)),
                      pl.BlockSpec(memory_space=pl.ANY)],
            out_specs=pl.BlockSpec((rows, d), lambda i, pos: (i, 0)),
            scratch_shapes=[pltpu.VMEM((2, TOP_K, rows, d), F32), pltpu.SemaphoreType.DMA((2,))],
        ),
        out_shape=jax.ShapeDtypeStruct((m, d), F32),
        compiler_params=_params("arbitrary"),
        name="combine",
    )(pos, x, gate.T, y)


def _routing_tables(idx, n_exp, tm, granule):
    m = idx.shape[1]
    n_assign = TOP_K * m
    n_tiles = n_assign // tm + n_exp
    p = n_tiles * tm
    e_flat = idx.reshape(-1)
    onehot = (e_flat[:, None] == jnp.arange(n_exp)[None, :]).astype(jnp.int32)
    before = jnp.cumsum(onehot, axis=0) - onehot
    rank = jnp.sum(before * onehot, axis=1)
    counts = jnp.sum(onehot, axis=0)
    padded = ((counts + tm - 1) // tm) * tm
    ends = jnp.cumsum(padded)
    offsets = ends - padded
    pos = offsets[e_flat] + rank
    token = jnp.tile(jnp.arange(m, dtype=jnp.int32), TOP_K)
    src_rows = jnp.zeros((p,), jnp.int32).at[pos].set(token)
    tile_start = jnp.arange(n_tiles, dtype=jnp.int32) * tm
    tile_expert = jnp.sum((tile_start[:, None] >= ends[None, :]).astype(jnp.int32), axis=1)
    tile_valid = (tile_start < ends[-1]).astype(jnp.int32)
    tile_expert = jnp.minimum(tile_expert, n_exp - 1).astype(jnp.int32)
    experts = jnp.arange(n_exp, dtype=jnp.int32)
    present = counts > 0
    later_present = (experts[None, :] > experts[:, None]) & present[None, :]
    next_present = jnp.min(jnp.where(later_present, experts[None, :], n_exp), axis=1)
    next_present = jnp.where(next_present == n_exp, -1, next_present).astype(jnp.int32)
    group_of_expert = (jnp.cumsum(present.astype(jnp.int32)) - 1).astype(jnp.int32)
    n_groups = jnp.sum(present.astype(jnp.int32)).reshape(1)
    tile_index = jnp.arange(n_tiles, dtype=jnp.int32)
    x_row_block = jnp.minimum(tile_index, jnp.sum(tile_valid) - 1).astype(jnp.int32)
    assigned = jnp.clip((offsets + counts)[tile_expert] - tile_start, 0, tm)
    tile_rows = (((assigned + granule - 1) // granule) * granule * tile_valid).astype(jnp.int32)
    tables = (tile_expert, tile_rows, group_of_expert[tile_expert], next_present[tile_expert], n_groups,
              x_row_block)
    return src_rows, pos.astype(jnp.int32), tables


def _even_layer(x, seq, norm1, w_in, pool_w, pool_scale, sg_norm, sg_w, sg_b, w_out, norm2, w1, w3, w2,
                tm=512, tm_wide=1024, tn=512, tn_up=256):
    m, d = x.shape
    pool_width = pool_scale.shape[0]
    tm_wide = min(tm_wide, m)
    wide = _dense_tables(m, tm_wide)
    h = rmsnorm(x, norm1)
    z = gmm(h, [w_in[None]], wide, tm=tm_wide, tn=tn, out_dtype=F32,
            epilogue=functools.partial(_ep_gelu_cols, pool_width // tn), name="w_in")
    y = mixer(z, pool_w, pool_scale, sg_norm, sg_w, sg_b, seq)
    x = gmm(y, [w_out[None]], wide, tm=tm_wide, tn=tn, out_dtype=F32, epilogue=_ep_residual,
            extras=[x], extra_specs=[_tile_spec(tm_wide, tn)], name="e_w_out")
    h = rmsnorm(x, norm2)
    hf = gmm(h, [w1[None], w3[None]], wide, tm=tm_wide, tn=tn_up, out_dtype=BF16, epilogue=_ep_swiglu,
             name="ffn_up")
    d_ff = w2.shape[0]
    w2h = w2.reshape(2, d_ff // 2, d)
    for half in range(2):
        x = gmm(hf, [w2h], _dense_tables(m, tm, half), tm=tm, tn=tn, out_dtype=F32, epilogue=_ep_residual,
                extras=[x], extra_specs=[_tile_spec(tm, tn)], x_col_block=half, name="ffn_down")
    return x


def _odd_layer(x, batch, seq, norm1, w_qkv, q_norm, k_norm, w_out, norm2, router, w1, w3, w2,
               tm=512, tm_wide=1024, tn=512):
    m, d = x.shape
    heads = d // SB_HEAD_DIM
    tm_wide = min(tm_wide, m)
    wide = _dense_tables(m, tm_wide)
    h = rmsnorm(x, norm1)
    gain_spec = pl.BlockSpec((1, SB_HEAD_DIM), lambda j, i, *_: (0, 0))
    q_gain = q_norm * (SB_HEAD_DIM ** -0.5 * LOG2E)
    qkv = gmm(h, [w_qkv[None]], wide, tm=tm_wide, tn=tn, out_dtype=BF16,
              epilogue=functools.partial(_ep_qkv, d // tn, d // tn),
              extras=[q_gain.reshape(1, -1), k_norm.reshape(1, -1)], extra_specs=[gain_spec, gain_spec],
              name="w_qkv")
    o = sb_attention(qkv, batch, seq, heads)
    x = gmm(o, [w_out[None]], wide, tm=tm_wide, tn=tn, out_dtype=F32, epilogue=_ep_residual,
            extras=[x], extra_specs=[_tile_spec(tm_wide, tn)], name="o_w_out")

    n_exp = router.shape[1]
    idx, gate, h_packed = router_top2(x, norm2, router)
    granule = tm // 4
    row_options = tuple(range(granule, tm + 1, granule))
    src_rows, pos, tables = _routing_tables(idx, n_exp, tm, granule)
    chunks = d // 2 // LANES
    xs = gather_rows(h_packed, src_rows, chunks)
    hs = gmm(xs, [w1, w3], tables, tm=tm, tn=tn, out_dtype=BF16, epilogue=_ep_swiglu,
             row_options=row_options, name="moe_up")
    ys = gmm(hs, [w2], tables, tm=tm, tn=tn, out_dtype=F32, epilogue=_ep_plain,
             row_options=row_options, name="moe_down")
    return combine(x, ys, pos, gate)


def kernel(x, e_norm1, e_w_in, e_pool_w, e_pool_scale, e_sg_norm, e_sg_w, e_sg_b, e_w_out, e_norm2, e_ffn_w1, e_ffn_w3, e_ffn_w2, o_norm1, o_w_qkv, o_q_norm, o_k_norm, o_w_out, o_norm2, o_router, o_moe_w1, o_moe_w3, o_moe_w2):
    batch, seq, d = x.shape
    depth = e_norm1.shape[0] + o_norm1.shape[0]
    xf = x.reshape(batch * seq, d)
    for layer in range(depth):
        i = layer // 2
        if layer % 2 == 0:
            xf = _even_layer(xf, seq, e_norm1[i], e_w_in[i], e_pool_w[i], e_pool_scale[i], e_sg_norm[i],
                             e_sg_w[i], e_sg_b[i], e_w_out[i], e_norm2[i], e_ffn_w1[i], e_ffn_w3[i], e_ffn_w2[i])
        else:
            xf = _odd_layer(xf, batch, seq, o_norm1[i], o_w_qkv[i], o_q_norm[i], o_k_norm[i], o_w_out[i],
                            o_norm2[i], o_router[i], o_moe_w1[i], o_moe_w3[i], o_moe_w2[i])
    return xf.reshape(batch, seq, d)
```

```python
import functools

import jax
import jax.numpy as jnp
from jax import lax
from jax.experimental import pallas as pl
from jax.experimental.pallas import tpu as pltpu

EPS = 1e-6
CHUNK = 64
SG_LEN = 128
SG_HEADS = 8
POOL_WINDOWS = (2, 4, 8, 16)
POOL_HALO = 16
SB_HEAD_DIM = 128
SB_EXIT_LOG2 = -150.0
LOG2E = 1.4426950408889634
TOP_K = 2

V7X_VMEM_BYTES = 64 * 1024 * 1024
VMEM_LIMIT_BYTES = V7X_VMEM_BYTES - 8 * 1024 * 1024
LANES = 128

F32 = jnp.float32
BF16 = jnp.bfloat16


def _params(*semantics):
    return pltpu.CompilerParams(dimension_semantics=semantics, vmem_limit_bytes=VMEM_LIMIT_BYTES)


def _rmsnorm_kernel(x_ref, g_ref, o_ref):
    x = x_ref[...]
    ms = jnp.mean(x * x, axis=-1, keepdims=True)
    o_ref[...] = (x * lax.rsqrt(ms + EPS) * g_ref[...]).astype(o_ref.dtype)


def rmsnorm(x, gain, tm=256):
    m, d = x.shape
    tm = min(tm, m)
    return pl.pallas_call(
        _rmsnorm_kernel,
        grid=(m // tm,),
        in_specs=[pl.BlockSpec((tm, d), lambda i: (i, 0)),
                  pl.BlockSpec((1, d), lambda i: (0, 0))],
        out_specs=pl.BlockSpec((tm, d), lambda i: (i, 0)),
        out_shape=jax.ShapeDtypeStruct((m, d), BF16),
        compiler_params=_params("arbitrary"),
        name="rmsnorm",
    )(x, gain.reshape(1, d))


def _gmm_kernel(te_ref, rows_ref, grp_ref, nexte_ref, ngroups_ref, xrow_ref, x_ref, *refs,
                n_w, n_extra, epilogue, tn, last_width, row_options):
    del xrow_ref
    w_hbm = refs[:n_w]
    extra_refs = refs[n_w:n_w + n_extra]
    o_ref = refs[n_w + n_extra]
    wbuf = refs[n_w + n_extra + 1:2 * n_w + n_extra + 1]
    sems = refs[2 * n_w + n_extra + 1]
    j = pl.program_id(0)
    i = pl.program_id(1)
    nj = pl.num_programs(0)
    valid = rows_ref[i] != 0
    starts_group = valid & ((i == 0) | (te_ref[i] != te_ref[jnp.maximum(i - 1, 0)]))
    slot = (j * ngroups_ref[0] + grp_ref[i]) % 2

    def block_copies(expert, jj, slot, width):
        col0 = pl.multiple_of(jj * tn, tn)
        return [pltpu.make_async_copy(w_hbm[k].at[expert, :, pl.ds(col0, width)],
                                      wbuf[k].at[slot, :, pl.ds(0, width)], sems.at[slot, k])
                for k in range(n_w)]

    def for_block(expert, jj, slot, action):
        if last_width == tn:
            for c in block_copies(expert, jj, slot, tn):
                action(c)
        else:
            @pl.when(jj < nj - 1)
            def _():
                for c in block_copies(expert, jj, slot, tn):
                    action(c)

            @pl.when(jj == nj - 1)
            def _():
                for c in block_copies(expert, jj, slot, last_width):
                    action(c)

    def start(c):
        c.start()

    def wait(c):
        c.wait()

    @pl.when(starts_group)
    def _():
        @pl.when((j == 0) & (i == 0))
        def _():
            for_block(te_ref[i], j, slot, start)

        for_block(te_ref[i], j, slot, wait)
        next_expert = nexte_ref[i]

        @pl.when(next_expert >= 0)
        def _():
            for_block(next_expert, j, 1 - slot, start)

        @pl.when((next_expert < 0) & (j + 1 < nj))
        def _():
            for_block(te_ref[0], j + 1, 1 - slot, start)

    tm = x_ref.shape[0]
    for rows in row_options:
        @pl.when(rows_ref[i] == rows)
        def _(rows=rows):
            x = x_ref[...] if rows == tm else x_ref[:rows, :]
            accs = [jnp.dot(x, wb[slot].astype(BF16), preferred_element_type=F32) for wb in wbuf]
            if rows == tm:
                epilogue(accs, extra_refs, o_ref, j)
            else:
                epilogue(accs, extra_refs, o_ref.at[pl.ds(0, rows), :], j)
                o_ref[rows:, :] = jnp.zeros((tm - rows, o_ref.shape[1]), o_ref.dtype)

    @pl.when(jnp.logical_not(valid))
    def _():
        o_ref[...] = jnp.zeros(o_ref.shape, o_ref.dtype)


def gmm(x, weights, tables, *, tm, tn, epilogue, out_dtype, extras=(), extra_specs=(), x_col_block=0,
        row_options=None, name="gmm"):
    p = x.shape[0]
    _, k, n = weights[0].shape
    nj = pl.cdiv(n, tn)
    n_w = len(weights)
    row_options = (tm,) if row_options is None else tuple(row_options)
    assert not extras or row_options == (tm,)
    kern = functools.partial(_gmm_kernel, n_w=n_w, n_extra=len(extras), epilogue=epilogue, tn=tn,
                             last_width=n - (nj - 1) * tn, row_options=row_options)
    return pl.pallas_call(
        kern,
        grid_spec=pltpu.PrefetchScalarGridSpec(
            num_scalar_prefetch=len(tables),
            grid=(nj, p // tm),
            in_specs=[pl.BlockSpec((tm, k), lambda j, i, te, tv, grp, ne, ng, xrow: (xrow[i], x_col_block))]
                     + [pl.BlockSpec(memory_space=pl.ANY)] * n_w + list(extra_specs),
            out_specs=pl.BlockSpec((tm, tn), lambda j, i, *_: (i, j)),
            scratch_shapes=[pltpu.VMEM((2, k, tn), F32)] * n_w + [pltpu.SemaphoreType.DMA((2, n_w))],
        ),
        out_shape=jax.ShapeDtypeStruct((p, n), out_dtype),
        compiler_params=_params("arbitrary", "arbitrary"),
        name=name,
    )(*tables, x, *weights, *extras)


def _dense_tables(m, tm, expert=0):
    n = m // tm
    zeros = jnp.zeros((n,), jnp.int32)
    return (jnp.full((n,), expert, jnp.int32), jnp.full((n,), tm, jnp.int32), zeros, zeros - 1,
            jnp.ones((1,), jnp.int32), jnp.arange(n, dtype=jnp.int32))


def _tile_spec(tm, tn):
    return pl.BlockSpec((tm, tn), lambda j, i, *_: (i, j))


def _ep_gelu_cols(first_gelu_block, accs, extra_refs, o_ref, j):
    del extra_refs
    acc = accs[0]

    @pl.when(j < first_gelu_block)
    def _():
        o_ref[...] = acc.astype(o_ref.dtype)

    @pl.when(j >= first_gelu_block)
    def _():
        o_ref[...] = jax.nn.gelu(acc).astype(o_ref.dtype)


def _ep_residual(accs, extra_refs, o_ref, j):
    del j
    o_ref[...] = (extra_refs[0][...] + accs[0]).astype(o_ref.dtype)


def _ep_swiglu(accs, extra_refs, o_ref, j):
    del extra_refs, j
    a, g = accs
    o_ref[...] = (jax.nn.silu(a) * g).astype(o_ref.dtype)


def _ep_plain(accs, extra_refs, o_ref, j):
    del extra_refs, j
    o_ref[...] = accs[0].astype(o_ref.dtype)


def _ep_qkv(q_blocks, k_blocks, accs, extra_refs, o_ref, j):
    acc = accs[0]
    qg_ref, kg_ref = extra_refs
    tn = acc.shape[1]

    @pl.when(j < q_blocks + k_blocks)
    def _():
        gain = jnp.where(j < q_blocks, qg_ref[...], kg_ref[...])
        for c in range(tn // SB_HEAD_DIM):
            sl = slice(c * SB_HEAD_DIM, (c + 1) * SB_HEAD_DIM)
            a = acc[:, sl]
            ms = jnp.mean(a * a, axis=-1, keepdims=True)
            o_ref[:, sl] = (a * lax.rsqrt(ms + EPS) * gain).astype(o_ref.dtype)

    @pl.when(j >= q_blocks + k_blocks)
    def _():
        o_ref[...] = acc.astype(o_ref.dtype)


def _mixer_kernel(za_ref, halo_ref, u_ref, v_ref, pw_ref, ps_ref, sgn_ref, sgw_ref, sgb_ref,
                  o_ref, zbuf, *, tiles_per_seq):
    i = pl.program_id(0)
    ts, pool_width = za_ref.shape
    sg_width = v_ref.shape[1]
    group_width = pool_width // len(POOL_WINDOWS)
    head_dim = sg_width // SG_HEADS
    tile_in_seq = i % tiles_per_seq

    za = za_ref[...]
    zbuf[POOL_HALO:POOL_HALO + ts, :] = za
    zbuf[0:POOL_HALO, :] = jnp.where(tile_in_seq == 0, 0.0, halo_ref[...])
    n_avail = (lax.broadcasted_iota(jnp.int32, (ts, 1), 0) + tile_in_seq * ts + 1).astype(F32)
    for g, w in enumerate(POOL_WINDOWS):
        cols = slice(g * group_width, (g + 1) * group_width)
        win = za[:, cols]
        for back in range(1, w):
            win = win + zbuf[POOL_HALO - back:POOL_HALO - back + ts, cols]
        mean = win / jnp.minimum(n_avail, float(w))
        pooled = (mean - za[:, cols]).astype(BF16)
        ya = jnp.dot(pooled, pw_ref[g], preferred_element_type=F32) * ps_ref[:, cols]
        o_ref[:, cols] = ya.astype(o_ref.dtype)

    v = v_ref[...]
    ms = jnp.mean(v * v, axis=-1, keepdims=True)
    vn = (v * lax.rsqrt(ms + EPS) * sgn_ref[...]).astype(BF16)
    pos_i = lax.broadcasted_iota(jnp.int32, (SG_LEN, SG_LEN), 0)
    pos_j = lax.broadcasted_iota(jnp.int32, (SG_LEN, SG_LEN), 1)
    allowed = (pos_j // CHUNK) <= (pos_i // CHUNK)
    for h in range(SG_HEADS):
        w_h = jnp.where(allowed, sgw_ref[h], 0.0).astype(BF16)
        hc = slice(h * head_dim, (h + 1) * head_dim)
        for c in range(ts // SG_LEN):
            rows = slice(c * SG_LEN, (c + 1) * SG_LEN)
            mixed = jnp.dot(w_h, vn[rows, hc], preferred_element_type=F32) + sgb_ref[:, hc]
            yb = u_ref[rows, hc] * mixed
            o_ref[rows, pool_width + h * head_dim:pool_width + (h + 1) * head_dim] = yb.astype(o_ref.dtype)


def mixer(z, pool_w, pool_scale, sg_norm, sg_w, sg_b, seq, ts=256):
    m = z.shape[0]
    pool_width = pool_scale.shape[0]
    sg_width = sg_norm.shape[0]
    assert pool_width == sg_width
    ts = min(ts, seq)
    tiles_per_seq = seq // ts
    halo_blocks_per_tile = ts // POOL_HALO
    head_dim = sg_width // SG_HEADS
    bias = jnp.repeat(sg_b.T, head_dim, axis=1)
    kern = functools.partial(_mixer_kernel, tiles_per_seq=tiles_per_seq)
    const2 = lambda i: (0, 0)
    const3 = lambda i: (0, 0, 0)
    return pl.pallas_call(
        kern,
        grid=(m // ts,),
        in_specs=[
            pl.BlockSpec((ts, pool_width), lambda i: (i, 0)),
            pl.BlockSpec((POOL_HALO, pool_width), lambda i: (jnp.maximum(i * halo_blocks_per_tile - 1, 0), 0)),
            pl.BlockSpec((ts, sg_width), lambda i: (i, 1)),
            pl.BlockSpec((ts, sg_width), lambda i: (i, 2)),
            pl.BlockSpec(pool_w.shape, const3),
            pl.BlockSpec((1, pool_width), const2),
            pl.BlockSpec((1, sg_width), const2),
            pl.BlockSpec(sg_w.shape, const3),
            pl.BlockSpec(bias.shape, const2),
        ],
        out_specs=pl.BlockSpec((ts, pool_width + sg_width), lambda i: (i, 0)),
        out_shape=jax.ShapeDtypeStruct((m, pool_width + sg_width), BF16),
        scratch_shapes=[pltpu.VMEM((ts + POOL_HALO, pool_width), F32)],
        compiler_params=_params("arbitrary"),
        name="mixer",
    )(z, z, z, z, pool_w.astype(BF16), pool_scale.reshape(1, -1), sg_norm.reshape(1, -1), sg_w, bias)


def _attn_kernel(q_ref, k_ref, v_ref, o_ref, *, group):
    qi = pl.program_id(2)
    tq = q_ref.shape[0]
    t_pos = lax.broadcasted_iota(jnp.int32, (tq, tq), 0)
    s_pos = lax.broadcasted_iota(jnp.int32, (tq, tq), 1)
    strict = s_pos < t_pos
    neg_tri = jnp.where(strict, -1.0, 0.0).astype(BF16)

    def head(g):
        return slice(g * SB_HEAD_DIM, (g + 1) * SB_HEAD_DIM)

    def sweep(k0, accs, carries, mask):
        zs, softpluses = [], []
        for g in range(group):
            z = lax.dot_general(q_ref[:, head(g)], k_ref[pl.ds(k0, tq), head(g)],
                                (((1,), (1,)), ((), ())), preferred_element_type=F32)
            neg_abs = lax.bitcast_convert_type(
                lax.bitcast_convert_type(z, jnp.uint32) | jnp.uint32(0x80000000), F32)
            softplus = jnp.maximum(z, 0.0) + jnp.log(1.0 + jnp.exp2(neg_abs)) * LOG2E
            if mask is not None:
                softplus = jnp.where(mask, softplus, 0.0)
            zs.append(z)
            softpluses.append(softplus)
        within = jnp.dot(jnp.concatenate([sp.astype(BF16) for sp in softpluses], axis=0), neg_tri,
                         preferred_element_type=F32)
        new_accs, new_carries = [], []
        for g in range(group):
            after = within[g * tq:(g + 1) * tq] + carries[g]
            attn = jnp.exp2(zs[g] - softpluses[g] + after)
            if mask is not None:
                attn = jnp.where(mask, attn, 0.0)
            out = jnp.dot(attn.astype(BF16), v_ref[pl.ds(k0, tq), head(g)], preferred_element_type=F32)
            new_accs.append(out if accs is None else accs[g] + out)
            new_carries.append(carries[g] - jnp.sum(softpluses[g], axis=1, keepdims=True))
        return tuple(new_accs), tuple(new_carries)

    def alive(carries):
        top = functools.reduce(jnp.maximum, carries)
        return (jnp.max(top) > SB_EXIT_LOG2).astype(jnp.int32)

    zero = jnp.zeros((tq, 1), F32)
    accs, carries = sweep(pl.multiple_of(qi * tq, tq), None, (zero,) * group, strict)

    def cond(state):
        n, live = state[0], state[1]
        return jnp.logical_and(n < qi, live > 0)

    def body(state):
        n, _, accs, carries = state
        accs, carries = sweep(pl.multiple_of((qi - 1 - n) * tq, tq), accs, carries, None)
        return n + 1, alive(carries), accs, carries

    _, _, accs, _ = lax.while_loop(cond, body, (jnp.int32(0), alive(carries), accs, carries))
    for g in range(group):
        o_ref[:, head(g)] = accs[g].astype(o_ref.dtype)


def sb_attention(qkv, batch, seq, heads, tq=256, group=8):
    m = qkv.shape[0]
    tq = min(tq, seq)
    group = min(group, heads)
    nq = seq // tq
    n_groups = heads // group
    width = group * SB_HEAD_DIM
    kern = functools.partial(_attn_kernel, group=group)
    return pl.pallas_call(
        kern,
        grid=(batch, n_groups, nq),
        in_specs=[
            pl.BlockSpec((tq, width), lambda b, h, i: (b * nq + i, h)),
            pl.BlockSpec((seq, width), lambda b, h, i: (b, n_groups + h)),
            pl.BlockSpec((seq, width), lambda b, h, i: (b, 2 * n_groups + h)),
        ],
        out_specs=pl.BlockSpec((tq, width), lambda b, h, i: (b * nq + i, h)),
        out_shape=jax.ShapeDtypeStruct((m, heads * SB_HEAD_DIM), BF16),
        compiler_params=_params("arbitrary", "arbitrary", "arbitrary"),
        name="sb_attention",
    )(qkv, qkv, qkv)


def _router_kernel(x_ref, g_ref, rt_ref, idx_ref, gate_ref, hp_ref):
    x = x_ref[...]
    tm, d = x.shape
    ms = jnp.mean(x * x, axis=-1, keepdims=True)
    h = x * lax.rsqrt(ms + EPS) * g_ref[...]
    logits = lax.dot_general(rt_ref[...], h, (((1,), (1,)), ((), ())),
                             precision=lax.Precision.HIGHEST, preferred_element_type=F32)
    n_exp = logits.shape[0]
    eid = lax.broadcasted_iota(jnp.int32, logits.shape, 0)
    m1 = jnp.max(logits, axis=0, keepdims=True)
    i1 = jnp.min(jnp.where(logits == m1, eid, n_exp), axis=0, keepdims=True)
    rest = jnp.where(eid == i1, -jnp.inf, logits)
    m2 = jnp.max(rest, axis=0, keepdims=True)
    i2 = jnp.min(jnp.where(rest == m2, eid, n_exp), axis=0, keepdims=True)
    e2 = jnp.exp(m2 - m1)
    g1 = 1.0 / (1.0 + e2)
    idx_ref[...] = jnp.concatenate([i1, i2], axis=0)
    gate_ref[...] = jnp.concatenate([g1, e2 * g1], axis=0)

    half = d // 2
    chunks = half // LANES
    lo = lax.bitcast_convert_type(h[:, :half].astype(BF16).astype(F32), jnp.uint32)
    hi = lax.bitcast_convert_type(h[:, half:].astype(BF16).astype(F32), jnp.uint32)
    words = hi | (lo >> 16)
    for c in range(chunks):
        hp_ref[pl.ds(c, tm, stride=chunks), :] = words[:, c * LANES:(c + 1) * LANES]


def router_top2(x, gain, router, tm=256):
    m, d = x.shape
    n_exp = router.shape[1]
    tm = min(tm, m)
    chunks = d // 2 // LANES
    return pl.pallas_call(
        _router_kernel,
        grid=(m // tm,),
        in_specs=[pl.BlockSpec((tm, d), lambda i: (i, 0)),
                  pl.BlockSpec((1, d), lambda i: (0, 0)),
                  pl.BlockSpec((n_exp, d), lambda i: (0, 0))],
        out_specs=[pl.BlockSpec((TOP_K, tm), lambda i: (0, i)),
                   pl.BlockSpec((TOP_K, tm), lambda i: (0, i)),
                   pl.BlockSpec((tm * chunks, LANES), lambda i: (i, 0))],
        out_shape=[jax.ShapeDtypeStruct((TOP_K, m), jnp.int32),
                   jax.ShapeDtypeStruct((TOP_K, m), F32),
                   jax.ShapeDtypeStruct((m * chunks, LANES), jnp.uint32)],
        compiler_params=_params("arbitrary"),
        name="router_top2",
    )(x, gain.reshape(1, d), router.T)


ISSUE_UNROLL = 8


def _row_start(row, height):
    if height == 1 or isinstance(row, int):
        return row * height
    return pl.multiple_of(row * height, height)


def _row_copy(src_hbm, src_row, dst_vmem, dst_row, sem, height):
    return pltpu.make_async_copy(src_hbm.at[pl.ds(_row_start(src_row, height), height), :],
                                 dst_vmem.at[pl.ds(_row_start(dst_row, height), height), :], sem)


def _issue_rows(src_hbm, index_of_row, dst_vmem, sem, height=1):
    def group(b, c):
        for u in range(ISSUE_UNROLL):
            r = b * ISSUE_UNROLL + u
            _row_copy(src_hbm, index_of_row(r), dst_vmem, r, sem, height).start(priority=u % 2)
        return c
    lax.fori_loop(0, dst_vmem.shape[0] // height // ISSUE_UNROLL, group, 0)


def _wait_rows(src_hbm, dst_vmem, sem, height=1):
    def one(r, c):
        _row_copy(src_hbm, 0, dst_vmem, r, sem, height).wait()
        return c
    lax.fori_loop(0, dst_vmem.shape[0] // height, one, 0, unroll=ISSUE_UNROLL)


def _unpack_rows(x_ref, tm, chunks):
    words = jnp.concatenate([x_ref[pl.ds(c, tm, stride=chunks), :] for c in range(chunks)], axis=1)
    lo = lax.bitcast_convert_type(words << 16, F32).astype(BF16)
    hi = lax.bitcast_convert_type(words & jnp.uint32(0xFFFF0000), F32).astype(BF16)
    return jnp.concatenate([lo, hi], axis=1)


def _gather_kernel(src_ref, x_hbm, o_ref, buf, sems, *, height):
    i = pl.program_id(0)
    rows = o_ref.shape[0]
    slot = i % 2

    def issue(tile, slot):
        _issue_rows(x_hbm, lambda r: src_ref[tile * rows + r], buf.at[slot], sems.at[slot], height)

    @pl.when(i == 0)
    def _():
        issue(0, 0)

    @pl.when(i + 1 < pl.num_programs(0))
    def _():
        issue(i + 1, 1 - slot)

    _wait_rows(x_hbm, buf.at[slot], sems.at[slot], height)
    o_ref[...] = _unpack_rows(buf.at[slot], rows, height)


def gather_rows(x, src_rows, height, rows=512):
    p = src_rows.shape[0]
    kern = functools.partial(_gather_kernel, height=height)
    return pl.pallas_call(
        kern,
        grid_spec=pltpu.PrefetchScalarGridSpec(
            num_scalar_prefetch=1,
            grid=(p // rows,),
            in_specs=[pl.BlockSpec(memory_space=pl.ANY)],
            out_specs=pl.BlockSpec((rows, 2 * LANES * height), lambda i, src: (i, 0)),
            scratch_shapes=[pltpu.VMEM((2, rows * height, LANES), x.dtype), pltpu.SemaphoreType.DMA((2,))],
        ),
        out_shape=jax.ShapeDtypeStruct((p, 2 * LANES * height), BF16),
        compiler_params=_params("arbitrary"),
        name="gather_rows",
    )(src_rows, x)


def _combine_kernel(pos_ref, x_ref, gate_ref, y_hbm, o_ref, buf, sems):
    i = pl.program_id(0)
    rows = x_ref.shape[0]
    n_tokens = pos_ref.shape[0] // TOP_K
    slot = i % 2

    def issue(tile, slot):
        for k in range(TOP_K):
            _issue_rows(y_hbm, lambda r: pos_ref[k * n_tokens + tile * rows + r],
                        buf.at[slot, k], sems.at[slot])

    @pl.when(i == 0)
    def _():
        issue(0, 0)

    @pl.when(i + 1 < pl.num_programs(0))
    def _():
        issue(i + 1, 1 - slot)

    for k in range(TOP_K):
        _wait_rows(y_hbm, buf.at[slot, k], sems.at[slot])
    out = x_ref[...]
    for k in range(TOP_K):
        out = out + gate_ref[:, k:k + 1] * buf[slot, k]
    o_ref[...] = out


def combine(x, y, pos, gate, rows=256):
    m, d = x.shape
    rows = min(rows, m)
    return pl.pallas_call(
        _combine_kernel,
        grid_spec=pltpu.PrefetchScalarGridSpec(
            num_scalar_prefetch=1,
            grid=(m // rows,),
            in_specs=[pl.BlockSpec((rows, d), lambda i, pos: (i, 0)),
                      pl.BlockSpec((rows, TOP_K), lambda i, pos: (i, 0)),
                      pl.BlockSpec(memory_space=pl.ANY)],
            out_specs=pl.BlockSpec((rows, d), lambda i, pos: (i, 0)),
            scratch_shapes=[pltpu.VMEM((2, TOP_K, rows, d), F32), pltpu.SemaphoreType.DMA((2,))],
        ),
        out_shape=jax.ShapeDtypeStruct((m, d), F32),
        compiler_params=_params("arbitrary"),
        name="combine",
    )(pos, x, gate.T, y)


def _routing_tables(idx, n_exp, tm, granule):
    m = idx.shape[1]
    n_assign = TOP_K * m
    n_tiles = n_assign // tm + n_exp
    p = n_tiles * tm
    e_flat = idx.reshape(-1)
    onehot = (e_flat[:, None] == jnp.arange(n_exp)[None, :]).astype(jnp.int32)
    before = jnp.cumsum(onehot, axis=0) - onehot
    rank = jnp.sum(before * onehot, axis=1)
    counts = jnp.sum(onehot, axis=0)
    padded = ((counts + tm - 1) // tm) * tm
    ends = jnp.cumsum(padded)
    offsets = ends - padded
    pos = offsets[e_flat] + rank
    token = jnp.tile(jnp.arange(m, dtype=jnp.int32), TOP_K)
    src_rows = jnp.zeros((p,), jnp.int32).at[pos].set(token)
    tile_start = jnp.arange(n_tiles, dtype=jnp.int32) * tm
    tile_expert = jnp.sum((tile_start[:, None] >= ends[None, :]).astype(jnp.int32), axis=1)
    tile_valid = (tile_start < ends[-1]).astype(jnp.int32)
    tile_expert = jnp.minimum(tile_expert, n_exp - 1).astype(jnp.int32)
    experts = jnp.arange(n_exp, dtype=jnp.int32)
    present = counts > 0
    later_present = (experts[None, :] > experts[:, None]) & present[None, :]
    next_present = jnp.min(jnp.where(later_present, experts[None, :], n_exp), axis=1)
    next_present = jnp.where(next_present == n_exp, -1, next_present).astype(jnp.int32)
    group_of_expert = (jnp.cumsum(present.astype(jnp.int32)) - 1).astype(jnp.int32)
    n_groups = jnp.sum(present.astype(jnp.int32)).reshape(1)
    tile_index = jnp.arange(n_tiles, dtype=jnp.int32)
    x_row_block = jnp.minimum(tile_index, jnp.sum(tile_valid) - 1).astype(jnp.int32)
    assigned = jnp.clip((offsets + counts)[tile_expert] - tile_start, 0, tm)
    tile_rows = (((assigned + granule - 1) // granule) * granule * tile_valid).astype(jnp.int32)
    tables = (tile_expert, tile_rows, group_of_expert[tile_expert], next_present[tile_expert], n_groups,
              x_row_block)
    return src_rows, pos.astype(jnp.int32), tables


def _even_layer(x, seq, norm1, w_in, pool_w, pool_scale, sg_norm, sg_w, sg_b, w_out, norm2, w1, w3, w2,
                tm=512, tm_wide=1024, tn=512, tm_up=2048, tn_up=256):
    m, d = x.shape
    pool_width = pool_scale.shape[0]
    tm_wide = min(tm_wide, m)
    wide = _dense_tables(m, tm_wide)
    h = rmsnorm(x, norm1)
    z = gmm(h, [w_in[None]], wide, tm=tm_wide, tn=tn, out_dtype=F32,
            epilogue=functools.partial(_ep_gelu_cols, pool_width // tn), name="w_in")
    y = mixer(z, pool_w, pool_scale, sg_norm, sg_w, sg_b, seq)
    x = gmm(y, [w_out[None]], wide, tm=tm_wide, tn=tn, out_dtype=F32, epilogue=_ep_residual,
            extras=[x], extra_specs=[_tile_spec(tm_wide, tn)], name="e_w_out")
    h = rmsnorm(x, norm2)
    tm_up = min(tm_up, m)
    hf = gmm(h, [w1[None], w3[None]], _dense_tables(m, tm_up), tm=tm_up, tn=tn_up, out_dtype=BF16,
             epilogue=_ep_swiglu, name="ffn_up")
    d_ff = w2.shape[0]
    w2h = w2.reshape(2, d_ff // 2, d)
    for half in range(2):
        x = gmm(hf, [w2h], _dense_tables(m, tm, half), tm=tm, tn=tn, out_dtype=F32, epilogue=_ep_residual,
                extras=[x], extra_specs=[_tile_spec(tm, tn)], x_col_block=half, name="ffn_down")
    return x


def _odd_layer(x, batch, seq, norm1, w_qkv, q_norm, k_norm, w_out, norm2, router, w1, w3, w2,
               tm=512, tm_wide=1024, tn=512):
    m, d = x.shape
    heads = d // SB_HEAD_DIM
    tm_wide = min(tm_wide, m)
    wide = _dense_tables(m, tm_wide)
    h = rmsnorm(x, norm1)
    gain_spec = pl.BlockSpec((1, SB_HEAD_DIM), lambda j, i, *_: (0, 0))
    q_gain = q_norm * (SB_HEAD_DIM ** -0.5 * LOG2E)
    qkv = gmm(h, [w_qkv[None]], wide, tm=tm_wide, tn=tn, out_dtype=BF16,
              epilogue=functools.partial(_ep_qkv, d // tn, d // tn),
              extras=[q_gain.reshape(1, -1), k_norm.reshape(1, -1)], extra_specs=[gain_spec, gain_spec],
              name="w_qkv")
    o = sb_attention(qkv, batch, seq, heads)
    x = gmm(o, [w_out[None]], wide, tm=tm_wide, tn=tn, out_dtype=F32, epilogue=_ep_residual,
            extras=[x], extra_specs=[_tile_spec(tm_wide, tn)], name="o_w_out")

    n_exp = router.shape[1]
    idx, gate, h_packed = router_top2(x, norm2, router)
    granule = tm // 4
    row_options = tuple(range(granule, tm + 1, granule))
    src_rows, pos, tables = _routing_tables(idx, n_exp, tm, granule)
    chunks = d // 2 // LANES
    xs = gather_rows(h_packed, src_rows, chunks)
    hs = gmm(xs, [w1, w3], tables, tm=tm, tn=tn, out_dtype=BF16, epilogue=_ep_swiglu,
             row_options=row_options, name="moe_up")
    ys = gmm(hs, [w2], tables, tm=tm, tn=tn, out_dtype=F32, epilogue=_ep_plain,
             row_options=row_options, name="moe_down")
    return combine(x, ys, pos, gate)


def kernel(x, e_norm1, e_w_in, e_pool_w, e_pool_scale, e_sg_norm, e_sg_w, e_sg_b, e_w_out, e_norm2, e_ffn_w1, e_ffn_w3, e_ffn_w2, o_norm1, o_w_qkv, o_q_norm, o_k_norm, o_w_out, o_norm2, o_router, o_moe_w1, o_moe_w3, o_moe_w2):
    batch, seq, d = x.shape
    depth = e_norm1.shape[0] + o_norm1.shape[0]
    xf = x.reshape(batch * seq, d)
    for layer in range(depth):
        i = layer // 2
        if layer % 2 == 0:
            xf = _even_layer(xf, seq, e_norm1[i], e_w_in[i], e_pool_w[i], e_pool_scale[i], e_sg_norm[i],
                             e_sg_w[i], e_sg_b[i], e_w_out[i], e_norm2[i], e_ffn_w1[i], e_ffn_w3[i], e_ffn_w2[i])
        else:
            xf = _odd_layer(xf, batch, seq, o_norm1[i], o_w_qkv[i], o_q_norm[i], o_k_norm[i], o_w_out[i],
                            o_norm2[i], o_router[i], o_moe_w1[i], o_moe_w3[i], o_moe_w2[i])
    return xf.reshape(batch, seq, d)
```

```python
import functools

import jax
import jax.numpy as jnp
from jax import lax
from jax.experimental import pallas as pl
from jax.experimental.pallas import tpu as pltpu

EPS = 1e-6
CHUNK = 64
SG_LEN = 128
SG_HEADS = 8
POOL_WINDOWS = (2, 4, 8, 16)
POOL_HALO = 16
SB_HEAD_DIM = 128
SB_EXIT_LOG2 = -150.0
LOG2E = 1.4426950408889634
TOP_K = 2

V7X_VMEM_BYTES = 64 * 1024 * 1024
VMEM_LIMIT_BYTES = V7X_VMEM_BYTES - 8 * 1024 * 1024
LANES = 128

F32 = jnp.float32
BF16 = jnp.bfloat16


def _params(*semantics):
    return pltpu.CompilerParams(dimension_semantics=semantics, vmem_limit_bytes=VMEM_LIMIT_BYTES)


def _rmsnorm_kernel(x_ref, g_ref, o_ref):
    x = x_ref[...]
    ms = jnp.mean(x * x, axis=-1, keepdims=True)
    o_ref[...] = (x * lax.rsqrt(ms + EPS) * g_ref[...]).astype(o_ref.dtype)


def rmsnorm(x, gain, tm=256):
    m, d = x.shape
    tm = min(tm, m)
    return pl.pallas_call(
        _rmsnorm_kernel,
        grid=(m // tm,),
        in_specs=[pl.BlockSpec((tm, d), lambda i: (i, 0)),
                  pl.BlockSpec((1, d), lambda i: (0, 0))],
        out_specs=pl.BlockSpec((tm, d), lambda i: (i, 0)),
        out_shape=jax.ShapeDtypeStruct((m, d), BF16),
        compiler_params=_params("arbitrary"),
        name="rmsnorm",
    )(x, gain.reshape(1, d))


def _gmm_kernel(te_ref, rows_ref, grp_ref, nexte_ref, ngroups_ref, xrow_ref, x_ref, *refs,
                n_w, n_extra, epilogue, tn, last_width, row_options):
    del xrow_ref
    w_hbm = refs[:n_w]
    extra_refs = refs[n_w:n_w + n_extra]
    o_ref = refs[n_w + n_extra]
    wbuf = refs[n_w + n_extra + 1:2 * n_w + n_extra + 1]
    sems = refs[2 * n_w + n_extra + 1]
    j = pl.program_id(0)
    i = pl.program_id(1)
    nj = pl.num_programs(0)
    valid = rows_ref[i] != 0
    starts_group = valid & ((i == 0) | (te_ref[i] != te_ref[jnp.maximum(i - 1, 0)]))
    slot = (j * ngroups_ref[0] + grp_ref[i]) % 2

    def block_copies(expert, jj, slot, width):
        col0 = pl.multiple_of(jj * tn, tn)
        return [pltpu.make_async_copy(w_hbm[k].at[expert, :, pl.ds(col0, width)],
                                      wbuf[k].at[slot, :, pl.ds(0, width)], sems.at[slot, k])
                for k in range(n_w)]

    def for_block(expert, jj, slot, action):
        if last_width == tn:
            for c in block_copies(expert, jj, slot, tn):
                action(c)
        else:
            @pl.when(jj < nj - 1)
            def _():
                for c in block_copies(expert, jj, slot, tn):
                    action(c)

            @pl.when(jj == nj - 1)
            def _():
                for c in block_copies(expert, jj, slot, last_width):
                    action(c)

    def start(c):
        c.start()

    def wait(c):
        c.wait()

    @pl.when(starts_group)
    def _():
        @pl.when((j == 0) & (i == 0))
        def _():
            for_block(te_ref[i], j, slot, start)

        for_block(te_ref[i], j, slot, wait)
        next_expert = nexte_ref[i]

        @pl.when(next_expert >= 0)
        def _():
            for_block(next_expert, j, 1 - slot, start)

        @pl.when((next_expert < 0) & (j + 1 < nj))
        def _():
            for_block(te_ref[0], j + 1, 1 - slot, start)

    tm = x_ref.shape[0]
    for rows in row_options:
        @pl.when(rows_ref[i] == rows)
        def _(rows=rows):
            x = x_ref[...] if rows == tm else x_ref[:rows, :]
            accs = [jnp.dot(x, wb[slot].astype(BF16), preferred_element_type=F32) for wb in wbuf]
            if rows == tm:
                epilogue(accs, extra_refs, o_ref, j)
            else:
                epilogue(accs, extra_refs, o_ref.at[pl.ds(0, rows), :], j)
                o_ref[rows:, :] = jnp.zeros((tm - rows, o_ref.shape[1]), o_ref.dtype)

    @pl.when(jnp.logical_not(valid))
    def _():
        o_ref[...] = jnp.zeros(o_ref.shape, o_ref.dtype)


def gmm(x, weights, tables, *, tm, tn, epilogue, out_dtype, extras=(), extra_specs=(), x_col_block=0,
        row_options=None, name="gmm"):
    p = x.shape[0]
    _, k, n = weights[0].shape
    nj = pl.cdiv(n, tn)
    n_w = len(weights)
    row_options = (tm,) if row_options is None else tuple(row_options)
    assert not extras or row_options == (tm,)
    kern = functools.partial(_gmm_kernel, n_w=n_w, n_extra=len(extras), epilogue=epilogue, tn=tn,
                             last_width=n - (nj - 1) * tn, row_options=row_options)
    return pl.pallas_call(
        kern,
        grid_spec=pltpu.PrefetchScalarGridSpec(
            num_scalar_prefetch=len(tables),
            grid=(nj, p // tm),
            in_specs=[pl.BlockSpec((tm, k), lambda j, i, te, tv, grp, ne, ng, xrow: (xrow[i], x_col_block))]
                     + [pl.BlockSpec(memory_space=pl.ANY)] * n_w + list(extra_specs),
            out_specs=pl.BlockSpec((tm, tn), lambda j, i, *_: (i, j)),
            scratch_shapes=[pltpu.VMEM((2, k, tn), F32)] * n_w + [pltpu.SemaphoreType.DMA((2, n_w))],
        ),
        out_shape=jax.ShapeDtypeStruct((p, n), out_dtype),
        compiler_params=_params("arbitrary", "arbitrary"),
        name=name,
    )(*tables, x, *weights, *extras)


def _dense_tables(m, tm, expert=0):
    n = m // tm
    zeros = jnp.zeros((n,), jnp.int32)
    return (jnp.full((n,), expert, jnp.int32), jnp.full((n,), tm, jnp.int32), zeros, zeros - 1,
            jnp.ones((1,), jnp.int32), jnp.arange(n, dtype=jnp.int32))


def _tile_spec(tm, tn):
    return pl.BlockSpec((tm, tn), lambda j, i, *_: (i, j))


def _ep_gelu_cols(first_gelu_block, accs, extra_refs, o_ref, j):
    del extra_refs
    acc = accs[0]

    @pl.when(j < first_gelu_block)
    def _():
        o_ref[...] = acc.astype(o_ref.dtype)

    @pl.when(j >= first_gelu_block)
    def _():
        o_ref[...] = jax.nn.gelu(acc).astype(o_ref.dtype)


def _ep_residual(accs, extra_refs, o_ref, j):
    del j
    o_ref[...] = (extra_refs[0][...] + accs[0]).astype(o_ref.dtype)


def _ep_swiglu(accs, extra_refs, o_ref, j):
    del extra_refs, j
    a, g = accs
    o_ref[...] = (jax.nn.silu(a) * g).astype(o_ref.dtype)


def _ep_plain(accs, extra_refs, o_ref, j):
    del extra_refs, j
    o_ref[...] = accs[0].astype(o_ref.dtype)


def _ep_qkv(q_blocks, k_blocks, accs, extra_refs, o_ref, j):
    acc = accs[0]
    qg_ref, kg_ref = extra_refs
    tn = acc.shape[1]

    @pl.when(j < q_blocks + k_blocks)
    def _():
        gain = jnp.where(j < q_blocks, qg_ref[...], kg_ref[...])
        for c in range(tn // SB_HEAD_DIM):
            sl = slice(c * SB_HEAD_DIM, (c + 1) * SB_HEAD_DIM)
            a = acc[:, sl]
            ms = jnp.mean(a * a, axis=-1, keepdims=True)
            o_ref[:, sl] = (a * lax.rsqrt(ms + EPS) * gain).astype(o_ref.dtype)

    @pl.when(j >= q_blocks + k_blocks)
    def _():
        o_ref[...] = acc.astype(o_ref.dtype)


def _mixer_kernel(za_ref, halo_ref, u_ref, v_ref, pw_ref, ps_ref, sgn_ref, sgw_ref, sgb_ref,
                  o_ref, zbuf, *, tiles_per_seq):
    i = pl.program_id(0)
    ts, pool_width = za_ref.shape
    sg_width = v_ref.shape[1]
    group_width = pool_width // len(POOL_WINDOWS)
    head_dim = sg_width // SG_HEADS
    tile_in_seq = i % tiles_per_seq

    za = za_ref[...]
    zbuf[POOL_HALO:POOL_HALO + ts, :] = za
    zbuf[0:POOL_HALO, :] = jnp.where(tile_in_seq == 0, 0.0, halo_ref[...])
    n_avail = (lax.broadcasted_iota(jnp.int32, (ts, 1), 0) + tile_in_seq * ts + 1).astype(F32)
    for g, w in enumerate(POOL_WINDOWS):
        cols = slice(g * group_width, (g + 1) * group_width)
        win = za[:, cols]
        for back in range(1, w):
            win = win + zbuf[POOL_HALO - back:POOL_HALO - back + ts, cols]
        mean = win / jnp.minimum(n_avail, float(w))
        pooled = (mean - za[:, cols]).astype(BF16)
        ya = jnp.dot(pooled, pw_ref[g], preferred_element_type=F32) * ps_ref[:, cols]
        o_ref[:, cols] = ya.astype(o_ref.dtype)

    v = v_ref[...]
    ms = jnp.mean(v * v, axis=-1, keepdims=True)
    vn = (v * lax.rsqrt(ms + EPS) * sgn_ref[...]).astype(BF16)
    pos_i = lax.broadcasted_iota(jnp.int32, (SG_LEN, SG_LEN), 0)
    pos_j = lax.broadcasted_iota(jnp.int32, (SG_LEN, SG_LEN), 1)
    allowed = (pos_j // CHUNK) <= (pos_i // CHUNK)
    for h in range(SG_HEADS):
        w_h = jnp.where(allowed, sgw_ref[h], 0.0).astype(BF16)
        hc = slice(h * head_dim, (h + 1) * head_dim)
        for c in range(ts // SG_LEN):
            rows = slice(c * SG_LEN, (c + 1) * SG_LEN)
            mixed = jnp.dot(w_h, vn[rows, hc], preferred_element_type=F32) + sgb_ref[:, hc]
            yb = u_ref[rows, hc] * mixed
            o_ref[rows, pool_width + h * head_dim:pool_width + (h + 1) * head_dim] = yb.astype(o_ref.dtype)


def mixer(z, pool_w, pool_scale, sg_norm, sg_w, sg_b, seq, ts=256):
    m = z.shape[0]
    pool_width = pool_scale.shape[0]
    sg_width = sg_norm.shape[0]
    assert pool_width == sg_width
    ts = min(ts, seq)
    tiles_per_seq = seq // ts
    halo_blocks_per_tile = ts // POOL_HALO
    head_dim = sg_width // SG_HEADS
    bias = jnp.repeat(sg_b.T, head_dim, axis=1)
    kern = functools.partial(_mixer_kernel, tiles_per_seq=tiles_per_seq)
    const2 = lambda i: (0, 0)
    const3 = lambda i: (0, 0, 0)
    return pl.pallas_call(
        kern,
        grid=(m // ts,),
        in_specs=[
            pl.BlockSpec((ts, pool_width), lambda i: (i, 0)),
            pl.BlockSpec((POOL_HALO, pool_width), lambda i: (jnp.maximum(i * halo_blocks_per_tile - 1, 0), 0)),
            pl.BlockSpec((ts, sg_width), lambda i: (i, 1)),
            pl.BlockSpec((ts, sg_width), lambda i: (i, 2)),
            pl.BlockSpec(pool_w.shape, const3),
            pl.BlockSpec((1, pool_width), const2),
            pl.BlockSpec((1, sg_width), const2),
            pl.BlockSpec(sg_w.shape, const3),
            pl.BlockSpec(bias.shape, const2),
        ],
        out_specs=pl.BlockSpec((ts, pool_width + sg_width), lambda i: (i, 0)),
        out_shape=jax.ShapeDtypeStruct((m, pool_width + sg_width), BF16),
        scratch_shapes=[pltpu.VMEM((ts + POOL_HALO, pool_width), F32)],
        compiler_params=_params("arbitrary"),
        name="mixer",
    )(z, z, z, z, pool_w.astype(BF16), pool_scale.reshape(1, -1), sg_norm.reshape(1, -1), sg_w, bias)


def _attn_kernel(q_ref, k_ref, v_ref, o_ref, *, group):
    qi = pl.program_id(2)
    tq = q_ref.shape[0]
    t_pos = lax.broadcasted_iota(jnp.int32, (tq, tq), 0)
    s_pos = lax.broadcasted_iota(jnp.int32, (tq, tq), 1)
    strict = s_pos < t_pos
    neg_tri = jnp.where(strict, -1.0, 0.0).astype(BF16)

    def head(g):
        return slice(g * SB_HEAD_DIM, (g + 1) * SB_HEAD_DIM)

    def sweep(k0, accs, carries, mask):
        zs, softpluses = [], []
        for g in range(group):
            z = lax.dot_general(q_ref[:, head(g)], k_ref[pl.ds(k0, tq), head(g)],
                                (((1,), (1,)), ((), ())), preferred_element_type=F32)
            softplus = jnp.maximum(z, 0.0) + jnp.log(1.0 + jnp.exp2(-jnp.abs(z))) * LOG2E
            if mask is not None:
                softplus = jnp.where(mask, softplus, 0.0)
            zs.append(z)
            softpluses.append(softplus)
        within = jnp.dot(jnp.concatenate([sp.astype(BF16) for sp in softpluses], axis=0), neg_tri,
                         preferred_element_type=F32)
        new_accs, new_carries = [], []
        for g in range(group):
            after = within[g * tq:(g + 1) * tq] + carries[g]
            attn = jnp.exp2(zs[g] - softpluses[g] + after)
            if mask is not None:
                attn = jnp.where(mask, attn, 0.0)
            out = jnp.dot(attn.astype(BF16), v_ref[pl.ds(k0, tq), head(g)], preferred_element_type=F32)
            new_accs.append(out if accs is None else accs[g] + out)
            new_carries.append(carries[g] - jnp.sum(softpluses[g], axis=1, keepdims=True))
        return tuple(new_accs), tuple(new_carries)

    def alive(carries):
        top = functools.reduce(jnp.maximum, carries)
        return (jnp.max(top) > SB_EXIT_LOG2).astype(jnp.int32)

    zero = jnp.zeros((tq, 1), F32)
    accs, carries = sweep(pl.multiple_of(qi * tq, tq), None, (zero,) * group, strict)

    def cond(state):
        n, live = state[0], state[1]
        return jnp.logical_and(n < qi, live > 0)

    def body(state):
        n, _, accs, carries = state
        accs, carries = sweep(pl.multiple_of((qi - 1 - n) * tq, tq), accs, carries, None)
        return n + 1, alive(carries), accs, carries

    _, _, accs, _ = lax.while_loop(cond, body, (jnp.int32(0), alive(carries), accs, carries))
    for g in range(group):
        o_ref[:, head(g)] = accs[g].astype(o_ref.dtype)


def sb_attention(qkv, batch, seq, heads, tq=256, group=8):
    m = qkv.shape[0]
    tq = min(tq, seq)
    group = min(group, heads)
    nq = seq // tq
    n_groups = heads // group
    width = group * SB_HEAD_DIM
    kern = functools.partial(_attn_kernel, group=group)
    return pl.pallas_call(
        kern,
        grid=(batch, n_groups, nq),
        in_specs=[
            pl.BlockSpec((tq, width), lambda b, h, i: (b * nq + i, h)),
            pl.BlockSpec((seq, width), lambda b, h, i: (b, n_groups + h)),
            pl.BlockSpec((seq, width), lambda b, h, i: (b, 2 * n_groups + h)),
        ],
        out_specs=pl.BlockSpec((tq, width), lambda b, h, i: (b * nq + i, h)),
        out_shape=jax.ShapeDtypeStruct((m, heads * SB_HEAD_DIM), BF16),
        compiler_params=_params("arbitrary", "arbitrary", "arbitrary"),
        name="sb_attention",
    )(qkv, qkv, qkv)


def _router_kernel(x_ref, g_ref, rt_ref, idx_ref, gate_ref, hp_ref):
    x = x_ref[...]
    tm, d = x.shape
    ms = jnp.mean(x * x, axis=-1, keepdims=True)
    h = x * lax.rsqrt(ms + EPS) * g_ref[...]
    h_hi32 = h.astype(BF16).astype(F32)
    h_hi, h_lo = h_hi32.astype(BF16), (h - h_hi32).astype(BF16)
    r = rt_ref[...]
    r_hi32 = r.astype(BF16).astype(F32)
    r_hi, r_lo = r_hi32.astype(BF16), (r - r_hi32).astype(BF16)
    nt = (((1,), (1,)), ((), ()))
    logits = (lax.dot_general(r_hi, h_hi, nt, preferred_element_type=F32)
              + lax.dot_general(r_hi, h_lo, nt, preferred_element_type=F32)
              + lax.dot_general(r_lo, h_hi, nt, preferred_element_type=F32))
    n_exp = logits.shape[0]
    eid = lax.broadcasted_iota(jnp.int32, logits.shape, 0)
    m1 = jnp.max(logits, axis=0, keepdims=True)
    i1 = jnp.min(jnp.where(logits == m1, eid, n_exp), axis=0, keepdims=True)
    rest = jnp.where(eid == i1, -jnp.inf, logits)
    m2 = jnp.max(rest, axis=0, keepdims=True)
    i2 = jnp.min(jnp.where(rest == m2, eid, n_exp), axis=0, keepdims=True)
    e2 = jnp.exp(m2 - m1)
    g1 = 1.0 / (1.0 + e2)
    idx_ref[...] = jnp.concatenate([i1, i2], axis=0)
    gate_ref[...] = jnp.concatenate([g1, e2 * g1], axis=0)

    chunks = d // LANES
    for c in range(chunks):
        hp_ref[pl.ds(c, tm, stride=chunks), :] = h_hi32[:, c * LANES:(c + 1) * LANES]


def router_top2(x, gain, router, tm=256):
    m, d = x.shape
    n_exp = router.shape[1]
    tm = min(tm, m)
    chunks = d // LANES
    return pl.pallas_call(
        _router_kernel,
        grid=(m // tm,),
        in_specs=[pl.BlockSpec((tm, d), lambda i: (i, 0)),
                  pl.BlockSpec((1, d), lambda i: (0, 0)),
                  pl.BlockSpec((n_exp, d), lambda i: (0, 0))],
        out_specs=[pl.BlockSpec((TOP_K, tm), lambda i: (0, i)),
                   pl.BlockSpec((TOP_K, tm), lambda i: (0, i)),
                   pl.BlockSpec((tm * chunks, LANES), lambda i: (i, 0))],
        out_shape=[jax.ShapeDtypeStruct((TOP_K, m), jnp.int32),
                   jax.ShapeDtypeStruct((TOP_K, m), F32),
                   jax.ShapeDtypeStruct((m * chunks, LANES), F32)],
        compiler_params=_params("arbitrary"),
        name="router_top2",
    )(x, gain.reshape(1, d), router.T)


ISSUE_UNROLL = 8


def _row_start(row, height):
    if height == 1 or isinstance(row, int):
        return row * height
    return pl.multiple_of(row * height, height)


def _row_copy(src_hbm, src_row, dst_vmem, dst_row, sem, height):
    return pltpu.make_async_copy(src_hbm.at[pl.ds(_row_start(src_row, height), height), :],
                                 dst_vmem.at[pl.ds(_row_start(dst_row, height), height), :], sem)


def _issue_rows(src_hbm, index_of_row, dst_vmem, sem, height=1):
    def group(b, c):
        for u in range(ISSUE_UNROLL):
            r = b * ISSUE_UNROLL + u
            _row_copy(src_hbm, index_of_row(r), dst_vmem, r, sem, height).start(priority=u % 2)
        return c
    lax.fori_loop(0, dst_vmem.shape[0] // height // ISSUE_UNROLL, group, 0)


def _wait_rows(src_hbm, dst_vmem, sem, height=1):
    def one(r, c):
        _row_copy(src_hbm, 0, dst_vmem, r, sem, height).wait()
        return c
    lax.fori_loop(0, dst_vmem.shape[0] // height, one, 0, unroll=ISSUE_UNROLL)


def _unflatten_rows(x_ref, tm, chunks):
    cols = [x_ref[pl.ds(c, tm, stride=chunks), :] for c in range(chunks)]
    return jnp.concatenate(cols, axis=1).astype(BF16)


def _gather_kernel(src_ref, x_hbm, o_ref, buf, sems, *, height):
    i = pl.program_id(0)
    rows = o_ref.shape[0]
    slot = i % 2

    def issue(tile, slot):
        _issue_rows(x_hbm, lambda r: src_ref[tile * rows + r], buf.at[slot], sems.at[slot], height)

    @pl.when(i == 0)
    def _():
        issue(0, 0)

    @pl.when(i + 1 < pl.num_programs(0))
    def _():
        issue(i + 1, 1 - slot)

    _wait_rows(x_hbm, buf.at[slot], sems.at[slot], height)
    o_ref[...] = _unflatten_rows(buf.at[slot], rows, height)


def gather_rows(x, src_rows, height, rows=256):
    p = src_rows.shape[0]
    kern = functools.partial(_gather_kernel, height=height)
    return pl.pallas_call(
        kern,
        grid_spec=pltpu.PrefetchScalarGridSpec(
            num_scalar_prefetch=1,
            grid=(p // rows,),
            in_specs=[pl.BlockSpec(memory_space=pl.ANY)],
            out_specs=pl.BlockSpec((rows, LANES * height), lambda i, src: (i, 0)),
            scratch_shapes=[pltpu.VMEM((2, rows * height, LANES), x.dtype), pltpu.SemaphoreType.DMA((2,))],
        ),
        out_shape=jax.ShapeDtypeStruct((p, LANES * height), BF16),
        compiler_params=_params("arbitrary"),
        name="gather_rows",
    )(src_rows, x)


def _combine_kernel(pos_ref, x_ref, gate_ref, y_hbm, o_ref, buf, sems):
    i = pl.program_id(0)
    rows = x_ref.shape[0]
    n_tokens = pos_ref.shape[0] // TOP_K
    slot = i % 2

    def issue(tile, slot):
        for k in range(TOP_K):
            _issue_rows(y_hbm, lambda r: pos_ref[k * n_tokens + tile * rows + r],
                        buf.at[slot, k], sems.at[slot])

    @pl.when(i == 0)
    def _():
        issue(0, 0)

    @pl.when(i + 1 < pl.num_programs(0))
    def _():
        issue(i + 1, 1 - slot)

    for k in range(TOP_K):
        _wait_rows(y_hbm, buf.at[slot, k], sems.at[slot])
    out = x_ref[...]
    for k in range(TOP_K):
        out = out + gate_ref[:, k:k + 1] * buf[slot, k]
    o_ref[...] = out


def combine(x, y, pos, gate, rows=128):
    m, d = x.shape
    rows = min(rows, m)
    return pl.pallas_call(
        _combine_kernel,
        grid_spec=pltpu.PrefetchScalarGridSpec(
            num_scalar_prefetch=1,
            grid=(m // rows,),
            in_specs=[pl.BlockSpec((rows, d), lambda i, pos: (i, 0)),
                      pl.BlockSpec((rows, TOP_K), lambda i, pos: (i, 0)),
                      pl.BlockSpec(memory_space=pl.ANY)],
            out_specs=pl.BlockSpec((rows, d), lambda i, pos: (i, 0)),
            scratch_shapes=[pltpu.VMEM((2, TOP_K, rows, d), F32), pltpu.SemaphoreType.DMA((2,))],
        ),
        out_shape=jax.ShapeDtypeStruct((m, d), F32),
        compiler_params=_params("arbitrary"),
        name="combine",
    )(pos, x, gate.T, y)


def _routing_tables(idx, n_exp, tm, granule):
    m = idx.shape[1]
    n_assign = TOP_K * m
    n_tiles = n_assign // tm + n_exp
    p = n_tiles * tm
    e_flat = idx.reshape(-1)
    onehot = (e_flat[:, None] == jnp.arange(n_exp)[None, :]).astype(jnp.int32)
    before = jnp.cumsum(onehot, axis=0) - onehot
    rank = jnp.sum(before * onehot, axis=1)
    counts = jnp.sum(onehot, axis=0)
    padded = ((counts + tm - 1) // tm) * tm
    ends = jnp.cumsum(padded)
    offsets = ends - padded
    pos = offsets[e_flat] + rank
    token = jnp.tile(jnp.arange(m, dtype=jnp.int32), TOP_K)
    src_rows = jnp.zeros((p,), jnp.int32).at[pos].set(token)
    tile_start = jnp.arange(n_tiles, dtype=jnp.int32) * tm
    tile_expert = jnp.sum((tile_start[:, None] >= ends[None, :]).astype(jnp.int32), axis=1)
    tile_valid = (tile_start < ends[-1]).astype(jnp.int32)
    tile_expert = jnp.minimum(tile_expert, n_exp - 1).astype(jnp.int32)
    experts = jnp.arange(n_exp, dtype=jnp.int32)
    present = counts > 0
    later_present = (experts[None, :] > experts[:, None]) & present[None, :]
    next_present = jnp.min(jnp.where(later_present, experts[None, :], n_exp), axis=1)
    next_present = jnp.where(next_present == n_exp, -1, next_present).astype(jnp.int32)
    group_of_expert = (jnp.cumsum(present.astype(jnp.int32)) - 1).astype(jnp.int32)
    n_groups = jnp.sum(present.astype(jnp.int32)).reshape(1)
    tile_index = jnp.arange(n_tiles, dtype=jnp.int32)
    x_row_block = jnp.minimum(tile_index, jnp.sum(tile_valid) - 1).astype(jnp.int32)
    assigned = jnp.clip((offsets + counts)[tile_expert] - tile_start, 0, tm)
    tile_rows = (((assigned + granule - 1) // granule) * granule * tile_valid).astype(jnp.int32)
    tables = (tile_expert, tile_rows, group_of_expert[tile_expert], next_present[tile_expert], n_groups,
              x_row_block)
    return src_rows, pos.astype(jnp.int32), tables


def _even_layer(x, seq, norm1, w_in, pool_w, pool_scale, sg_norm, sg_w, sg_b, w_out, norm2, w1, w3, w2,
                tm=512, tm_wide=1024, tn=512, tm_up=2048, tn_up=256):
    m, d = x.shape
    pool_width = pool_scale.shape[0]
    tm_wide = min(tm_wide, m)
    wide = _dense_tables(m, tm_wide)
    h = rmsnorm(x, norm1)
    z = gmm(h, [w_in[None]], wide, tm=tm_wide, tn=tn, out_dtype=F32,
            epilogue=functools.partial(_ep_gelu_cols, pool_width // tn), name="w_in")
    y = mixer(z, pool_w, pool_scale, sg_norm, sg_w, sg_b, seq)
    x = gmm(y, [w_out[None]], wide, tm=tm_wide, tn=tn, out_dtype=F32, epilogue=_ep_residual,
            extras=[x], extra_specs=[_tile_spec(tm_wide, tn)], name="e_w_out")
    h = rmsnorm(x, norm2)
    tm_up = min(tm_up, m)
    hf = gmm(h, [w1[None], w3[None]], _dense_tables(m, tm_up), tm=tm_up, tn=tn_up, out_dtype=BF16,
             epilogue=_ep_swiglu, name="ffn_up")
    d_ff = w2.shape[0]
    w2h = w2.reshape(2, d_ff // 2, d)
    for half in range(2):
        x = gmm(hf, [w2h], _dense_tables(m, tm, half), tm=tm, tn=tn, out_dtype=F32, epilogue=_ep_residual,
                extras=[x], extra_specs=[_tile_spec(tm, tn)], x_col_block=half, name="ffn_down")
    return x


def _odd_layer(x, batch, seq, norm1, w_qkv, q_norm, k_norm, w_out, norm2, router, w1, w3, w2,
               tm=512, tm_wide=1024, tn=512):
    m, d = x.shape
    heads = d // SB_HEAD_DIM
    tm_wide = min(tm_wide, m)
    wide = _dense_tables(m, tm_wide)
    h = rmsnorm(x, norm1)
    gain_spec = pl.BlockSpec((1, SB_HEAD_DIM), lambda j, i, *_: (0, 0))
    q_gain = q_norm * (SB_HEAD_DIM ** -0.5 * LOG2E)
    qkv = gmm(h, [w_qkv[None]], wide, tm=tm_wide, tn=tn, out_dtype=BF16,
              epilogue=functools.partial(_ep_qkv, d // tn, d // tn),
              extras=[q_gain.reshape(1, -1), k_norm.reshape(1, -1)], extra_specs=[gain_spec, gain_spec],
              name="w_qkv")
    o = sb_attention(qkv, batch, seq, heads)
    x = gmm(o, [w_out[None]], wide, tm=tm_wide, tn=tn, out_dtype=F32, epilogue=_ep_residual,
            extras=[x], extra_specs=[_tile_spec(tm_wide, tn)], name="o_w_out")

    n_exp = router.shape[1]
    idx, gate, h_rows = router_top2(x, norm2, router)
    granule = tm // 4
    row_options = tuple(range(granule, tm + 1, granule))
    src_rows, pos, tables = _routing_tables(idx, n_exp, tm, granule)
    xs = gather_rows(h_rows, src_rows, d // LANES)
    hs = gmm(xs, [w1, w3], tables, tm=tm, tn=tn, out_dtype=BF16, epilogue=_ep_swiglu,
             row_options=row_options, name="moe_up")
    ys = gmm(hs, [w2], tables, tm=tm, tn=tn, out_dtype=F32, epilogue=_ep_plain,
             row_options=row_options, name="moe_down")
    return combine(x, ys, pos, gate)


def kernel(x, e_norm1, e_w_in, e_pool_w, e_pool_scale, e_sg_norm, e_sg_w, e_sg_b, e_w_out, e_norm2, e_ffn_w1, e_ffn_w3, e_ffn_w2, o_norm1, o_w_qkv, o_q_norm, o_k_norm, o_w_out, o_norm2, o_router, o_moe_w1, o_moe_w3, o_moe_w2):
    batch, seq, d = x.shape
    depth = e_norm1.shape[0] + o_norm1.shape[0]
    xf = x.reshape(batch * seq, d)
    for layer in range(depth):
        i = layer // 2
        if layer % 2 == 0:
            xf = _even_layer(xf, seq, e_norm1[i], e_w_in[i], e_pool_w[i], e_pool_scale[i], e_sg_norm[i],
                             e_sg_w[i], e_sg_b[i], e_w_out[i], e_norm2[i], e_ffn_w1[i], e_ffn_w3[i], e_ffn_w2[i])
        else:
            xf = _odd_layer(xf, batch, seq, o_norm1[i], o_w_qkv[i], o_q_norm[i], o_k_norm[i], o_w_out[i],
                            o_norm2[i], o_router[i], o_moe_w1[i], o_moe_w3[i], o_moe_w2[i])
    return xf.reshape(batch, seq, d)
```
